```python
import math
import jax, jax.numpy as jnp
from jax import lax
import numpy as np

D_MODEL = 4096
BATCH = 4
SEQ = 2048
DEPTH = 1
DEC_BATCH = 128
DEC_SEQ = 8
PAST_LEN = 16384
PAGE_SIZE = 128

MIX_WIDTH = D_MODEL
D_A = MIX_WIDTH // 2
D_B = MIX_WIDTH - D_A
HGRN_EXPAND = 128
H_A = D_A // HGRN_EXPAND
DK = HGRN_EXPAND
DV = D_A // H_A
HGRN_CHUNK = 64
CONV_W = 31
IN_COLS = 4 * D_A + 2 * D_B
PEER_HEADS = 8
PEER_NKEYS = 128
PEER_EXPERTS = PEER_NKEYS * PEER_NKEYS
PEER_QDIM = 256
PEER_HALF = PEER_QDIM // 2
PEER_TOPK = 16
PEER_BLOCK = 32
PLE_DIM = 256
EPS = 1e-6

kernel_name = 'hymba_hgrn2_conformer_peer_step'


def _rmsnorm(x, g):
    xf = x.astype(jnp.float32)
    out = xf * lax.rsqrt(jnp.mean(xf * xf, axis=-1, keepdims=True) + EPS) * g.astype(jnp.float32)
    return out.astype(x.dtype)


def _layernorm(x, g, b):
    xf = x.astype(jnp.float32)
    mu = jnp.mean(xf, axis=-1, keepdims=True)
    xc = xf - mu
    out = xc * lax.rsqrt(jnp.mean(xc * xc, axis=-1, keepdims=True) + EPS)
    return (out * g.astype(jnp.float32) + b.astype(jnp.float32)).astype(x.dtype)


def _hgrn2_scan(q, k, v, logf, s0):
    b, l = q.shape[0], q.shape[1]
    c = math.gcd(l, HGRN_CHUNK)
    n = l // c

    def to_chunks(t):
        return t.reshape(b, n, c, t.shape[2], t.shape[3]).transpose(1, 0, 3, 2, 4)

    causal = jnp.tril(jnp.ones((c, c), dtype=bool))[None, None, :, :, None]

    def step(s, inp):
        qc, kc, vc, lc = inp
        g = jnp.cumsum(lc, axis=2)
        o_inter = jnp.einsum('bhtk,bhkv->bhtv', qc * jnp.exp(g), s)
        diff = g[:, :, :, None, :] - g[:, :, None, :, :]
        decay = jnp.exp(jnp.where(causal, diff, -jnp.inf))
        a = jnp.einsum('bhtk,bhsk,bhtsk->bhts', qc, kc, decay)
        o = o_inter + jnp.einsum('bhts,bhsv->bhtv', a, vc)
        g_last = g[:, :, -1:, :]
        s_new = jnp.exp(g_last[:, :, 0, :])[..., None] * s + jnp.einsum(
            'bhsk,bhsv->bhkv', kc * jnp.exp(g_last - g), vc)
        return s_new, o

    s_fin, o = lax.scan(step, s0, (to_chunks(q), to_chunks(k), to_chunks(v), to_chunks(logf)))
    o = o.transpose(1, 0, 3, 2, 4).reshape(b, l, q.shape[2], v.shape[3])
    return o, s_fin


def _peer(h, wq, keys, u_tab, v_tab):
    b, l, d = h.shape
    t = b * l
    xt = h.reshape(t, d)
    qh = (xt @ wq).reshape(t, PEER_HEADS, 2, PEER_HALF).astype(jnp.float32)
    sc = jnp.einsum('thcd,hcnd->thcn', qh, keys.astype(jnp.float32))
    v1, i1 = lax.top_k(sc[:, :, 0], PEER_TOPK)
    v2, i2 = lax.top_k(sc[:, :, 1], PEER_TOPK)
    cand = (v1[..., :, None] + v2[..., None, :]).reshape(t, PEER_HEADS, PEER_TOPK * PEER_TOPK)
    cidx = (i1[..., :, None] * PEER_NKEYS + i2[..., None, :]).reshape(t, PEER_HEADS, PEER_TOPK * PEER_TOPK)
    top, pos = lax.top_k(cand, PEER_TOPK)
    eidx = jnp.take_along_axis(cidx, pos, axis=-1)
    gate = jax.nn.softmax(top, axis=-1)
    blk = math.gcd(t, PEER_BLOCK)
    nb = t // blk

    def block(args):
        xb, eb, gb = args
        ub = u_tab[eb]
        hid = jnp.einsum('td,thkd->thk', xb, ub).astype(jnp.float32)
        act = (gb * jax.nn.gelu(hid, approximate=False)).astype(xb.dtype)
        vb = v_tab[eb]
        return jnp.einsum('thk,thkd->td', act, vb)

    out = lax.map(block, (xt.reshape(nb, blk, d),
                          eidx.reshape(nb, blk, PEER_HEADS, PEER_TOPK),
                          gate.reshape(nb, blk, PEER_HEADS, PEER_TOPK)))
    return out.reshape(b, l, d)


def _layer(x, p, s0, buf, lb, norm1_g, w_in, hgrn_norm_g, conv_w, conv_b, conv_ln_g, conv_ln_b,
           w_out, norm2_g, peer_wq, peer_keys, peer_u, peer_v, ple_norm_g, ple_wg, ple_wp):
    b, l, _ = x.shape
    hn = _rmsnorm(x, norm1_g)
    z = hn @ w_in
    q_raw, f_raw, i_raw, g_raw, a_raw, b_raw = jnp.split(
        z, [D_A, 2 * D_A, 3 * D_A, 4 * D_A, 4 * D_A + D_B], axis=-1)

    f = lb + (1.0 - lb) * jax.nn.sigmoid(f_raw.astype(jnp.float32))
    logf = jnp.log(f)
    kk = 1.0 - f
    q = jax.nn.silu(q_raw.astype(jnp.float32))
    heads = lambda tt: tt.reshape(b, l, H_A, -1)
    o, s_new = _hgrn2_scan(heads(q), heads(kk), heads(i_raw.astype(jnp.float32)), heads(logf),
                           s0.astype(jnp.float32))
    o = _rmsnorm(o, hgrn_norm_g)
    o_a = (o.reshape(b, l, D_A) * jax.nn.silu(g_raw.astype(jnp.float32))).astype(x.dtype)

    u = a_raw * jax.nn.sigmoid(b_raw)
    up = jnp.concatenate([buf.astype(u.dtype), u], axis=1)
    c = lax.conv_general_dilated(up, conv_w[:, None, :].astype(u.dtype), (1,), 'VALID',
                                 dimension_numbers=('NWC', 'WIO', 'NWC'),
                                 feature_group_count=D_B) + conv_b
    buf_new = up[:, up.shape[1] - (CONV_W - 1):]
    o_b = jax.nn.silu(_layernorm(c, conv_ln_g, conv_ln_b)).astype(x.dtype)

    x = x + jnp.concatenate([o_a, o_b], axis=-1) @ w_out
    x = x + _peer(_rmsnorm(x, norm2_g), peer_wq, peer_keys, peer_u, peer_v)
    gate = jax.nn.sigmoid(_rmsnorm(x, ple_norm_g) @ ple_wg)
    x = x + gate * (p @ ple_wp)
    return x, s_new, buf_new


def setup_inputs(seed: int = 0) -> dict:
    key = jax.random.key(seed)
    ks = jax.random.split(key, 24)
    nrm = lambda k, shape, s: jax.random.normal(k, shape, jnp.float32) * s
    return {
        'x_prompt': nrm(ks[0], (BATCH, SEQ, D_MODEL), 1.0),
        'x_sample': nrm(ks[1], (DEC_BATCH, DEC_SEQ, D_MODEL), 1.0),
        'state_hgrn': nrm(ks[2], (DEPTH, DEC_BATCH, H_A, DK, DV), 0.5),
        'state_conv': nrm(ks[3], (DEPTH, DEC_BATCH, CONV_W - 1, D_B), 0.5),
        'p_prompt': nrm(ks[4], (DEPTH, BATCH, SEQ, PLE_DIM), 1.0),
        'p_sample': nrm(ks[5], (DEPTH, DEC_BATCH, DEC_SEQ, PLE_DIM), 1.0),
        'lb_logits': nrm(ks[6], (DEPTH + 1, D_A), 1.0),
        'norm1_g': 1.0 + nrm(ks[7], (DEPTH, D_MODEL), 0.01),
        'w_in': nrm(ks[8], (DEPTH, D_MODEL, IN_COLS), D_MODEL ** -0.5),
        'hgrn_norm_g': 1.0 + nrm(ks[9], (DEPTH, H_A, DV), 0.01),
        'conv_w': nrm(ks[10], (DEPTH, CONV_W, D_B), CONV_W ** -0.5),
        'conv_b': nrm(ks[11], (DEPTH, D_B), 0.01),
        'conv_ln_g': 1.0 + nrm(ks[12], (DEPTH, D_B), 0.01),
        'conv_ln_b': nrm(ks[13], (DEPTH, D_B), 0.01),
        'w_out': nrm(ks[14], (DEPTH, MIX_WIDTH, D_MODEL), MIX_WIDTH ** -0.5),
        'norm2_g': 1.0 + nrm(ks[15], (DEPTH, D_MODEL), 0.01),
        'peer_wq': nrm(ks[16], (DEPTH, D_MODEL, PEER_HEADS * PEER_QDIM), D_MODEL ** -0.5),
        'peer_keys': nrm(ks[17], (DEPTH, PEER_HEADS, 2, PEER_NKEYS, PEER_HALF), PEER_HALF ** -0.5),
        'peer_u': nrm(ks[18], (DEPTH, PEER_EXPERTS, D_MODEL), D_MODEL ** -0.5),
        'peer_v': nrm(ks[19], (DEPTH, PEER_EXPERTS, D_MODEL), 0.5),
        'ple_norm_g': 1.0 + nrm(ks[20], (DEPTH, D_MODEL), 0.01),
        'ple_wg': nrm(ks[21], (DEPTH, D_MODEL, D_MODEL), D_MODEL ** -0.5),
        'ple_wp': nrm(ks[22], (DEPTH, PLE_DIM, D_MODEL), PLE_DIM ** -0.5),
        'final_g': 1.0 + nrm(ks[23], (D_MODEL,), 0.01),
    }


def reference(x_prompt, x_sample, state_hgrn, state_conv, p_prompt, p_sample, lb_logits, norm1_g,
              w_in, hgrn_norm_g, conv_w, conv_b, conv_ln_g, conv_ln_b, w_out, norm2_g, peer_wq,
              peer_keys, peer_u, peer_v, ple_norm_g, ple_wg, ple_wp, final_g):
    lb_all = jnp.cumsum(jax.nn.softmax(lb_logits.astype(jnp.float32), axis=0), axis=0)
    bp = x_prompt.shape[0]
    s0_prompt = jnp.zeros((bp, H_A, DK, DV), jnp.float32)
    buf_prompt = jnp.zeros((bp, CONV_W - 1, D_B), x_prompt.dtype)
    hp, hs = x_prompt, x_sample
    sp_list, cp_list, ss_list, cs_list = [], [], [], []
    for i in range(DEPTH):
        w = (norm1_g[i], w_in[i], hgrn_norm_g[i], conv_w[i], conv_b[i], conv_ln_g[i], conv_ln_b[i],
             w_out[i], norm2_g[i], peer_wq[i], peer_keys[i], peer_u[i], peer_v[i], ple_norm_g[i],
             ple_wg[i], ple_wp[i])
        hp, sp, cp = _layer(hp, p_prompt[i], s0_prompt, buf_prompt, lb_all[i], *w)
        hs, ss, cs = _layer(hs, p_sample[i], state_hgrn[i], state_conv[i], lb_all[i], *w)
        sp_list.append(sp.astype(state_hgrn.dtype))
        cp_list.append(cp.astype(state_conv.dtype))
        ss_list.append(ss.astype(state_hgrn.dtype))
        cs_list.append(cs.astype(state_conv.dtype))
    y_prompt = _rmsnorm(hp, final_g)
    y_sample = _rmsnorm(hs, final_g)
    new_hgrn_prompt = jnp.stack(sp_list)
    new_conv_prompt = jnp.stack(cp_list)
    new_hgrn_sample = jnp.stack(ss_list)
    new_conv_sample = jnp.stack(cs_list)
    return (y_prompt, y_sample, new_hgrn_prompt, new_conv_prompt, new_hgrn_sample, new_conv_sample)
```

```python
import functools
import math

import numpy as np
import jax
import jax.numpy as jnp
from jax import lax
from jax.experimental import pallas as pl
from jax.experimental.pallas import tpu as pltpu

EPS = 1e-6
HGRN_CHUNK = 64
PEER_TOPK = 16
LANES = 128
VMEM_LIMIT = 56 * 1024 * 1024
HIGHEST = lax.Precision.HIGHEST
F32 = jnp.float32
BF16 = jnp.bfloat16
NT_DIMS = (((1,), (1,)), ((), ()))
TN_DIMS = (((0,), (0,)), ((), ()))


def _params(*sem):
    return pltpu.CompilerParams(dimension_semantics=sem, vmem_limit_bytes=VMEM_LIMIT)


def _rmsnorm_kernel(x_ref, g_ref, o_ref):
    x = x_ref[...]
    inv = lax.rsqrt(jnp.mean(x * x, axis=-1, keepdims=True) + EPS)
    o_ref[...] = (x * inv * g_ref[...]).astype(o_ref.dtype)


def _rmsnorm(x, g, out_dtype, *, row0=0, rows=None, tm=256):
    t, d = x.shape
    rows = t - row0 if rows is None else rows
    assert rows % tm == 0 and row0 % tm == 0
    off = row0 // tm
    return pl.pallas_call(
        _rmsnorm_kernel,
        grid=(rows // tm,),
        in_specs=[pl.BlockSpec((tm, d), lambda i: (i + off, 0)),
                  pl.BlockSpec((1, d), lambda i: (0, 0))],
        out_specs=pl.BlockSpec((tm, d), lambda i: (i, 0)),
        out_shape=jax.ShapeDtypeStruct((rows, d), out_dtype),
        compiler_params=_params("parallel"),
        name="rmsnorm",
    )(x, g.reshape(1, d))


def _mm_kernel(a_ref, w_ref, o_ref):
    o_ref[...] = jnp.dot(a_ref[...], w_ref[...], preferred_element_type=F32)


def _mm_res_kernel(a_ref, w_ref, r_ref, o_ref):
    o_ref[...] = r_ref[...] + jnp.dot(a_ref[...], w_ref[...], preferred_element_type=F32)


def _mm_ple_kernel(a_ref, w_ref, p_ref, wp_ref, r_ref, o_ref):
    gate = jax.nn.sigmoid(jnp.dot(a_ref[...], w_ref[...], preferred_element_type=F32))
    proj = jnp.dot(p_ref[...], wp_ref[...], preferred_element_type=F32)
    o_ref[...] = r_ref[...] + gate * proj


def _matmul(a, w, *, res=None, ple=None, tm=1024, tn=1024):
    m, k = a.shape
    n = w.shape[1]
    assert m % tm == 0 and n % tn == 0
    in_specs = [pl.BlockSpec((tm, k), lambda i, j: (i, 0)),
                pl.BlockSpec((k, tn), lambda i, j: (0, j))]
    args = [a, w]
    body = _mm_kernel
    if ple is not None:
        p, wp = ple
        in_specs += [pl.BlockSpec((tm, p.shape[1]), lambda i, j: (i, 0)),
                     pl.BlockSpec((p.shape[1], tn), lambda i, j: (0, j))]
        args += [p, wp]
        body = _mm_ple_kernel
    elif res is not None:
        body = _mm_res_kernel
    if res is not None:
        in_specs.append(pl.BlockSpec((tm, tn), lambda i, j: (i, j)))
        args.append(res)
    return pl.pallas_call(
        body,
        grid=(m // tm, n // tn),
        in_specs=in_specs,
        out_specs=pl.BlockSpec((tm, tn), lambda i, j: (i, j)),
        out_shape=jax.ShapeDtypeStruct((m, n), F32),
        compiler_params=_params("parallel", "parallel"),
        name="matmul",
    )(*args)


def _scan_consts(rows, seq_len):
    r = np.arange(rows)
    seq = r // seq_len
    same_seq = seq[:, None] == seq[None, :]
    le = r[None, :] <= r[:, None]
    mats = [(le & same_seq), ((r[None, :] > r[:, None]) & same_seq)]
    masks = []
    m = seq_len
    while m >= 2:
        half = m // 2
        pos = r % m
        ref = r - pos + half - 1
        upper = pos >= half
        u = r[None, :]
        mq = (u > ref[:, None]) & (u <= r[:, None])
        mk = (u > r[:, None]) & (u <= ref[:, None])
        mats.append(np.where(upper[:, None], mq, mk))
        same_blk = (r // m)[:, None] == (r // m)[None, :]
        masks.append(same_blk & upper[:, None] & (~upper)[None, :])
        m = half
    cmat = np.concatenate(mats, axis=0).astype(np.float32)
    masks = np.stack(masks).astype(np.float32)
    eye = np.eye(rows, dtype=np.float32)
    return cmat, masks, eye


def _scan_kernel(*refs, rows, nseq, n_chunks, hb, n_levels, has_s0):
    if has_s0:
        (q_ref, f_ref, i_ref, g_ref, lb_ref, ng_ref, cmat_ref, mask_ref, eye_ref, s0_ref,
         o_ref, sfin_ref, s_scr) = refs
    else:
        (q_ref, f_ref, i_ref, g_ref, lb_ref, ng_ref, cmat_ref, mask_ref, eye_ref,
         o_ref, sfin_ref, s_scr) = refs
    t = pl.program_id(2)
    dk = LANES
    seq_len = rows // nseq

    @pl.when(t == 0)
    def _():
        if has_s0:
            for j in range(nseq):
                for hh in range(hb):
                    s_scr[j * hb + hh] = s0_ref[j, hh]
        else:
            s_scr[...] = jnp.zeros_like(s_scr)

    ones_blk = jnp.ones((seq_len, dk), F32)

    def chunk(ci, carry):
        r0 = pl.multiple_of(ci * rows, rows)
        for hh in range(hb):
            cols = slice(hh * dk, (hh + 1) * dk)
            qr = q_ref[pl.ds(r0, rows), cols]
            fr = f_ref[pl.ds(r0, rows), cols]
            v = i_ref[pl.ds(r0, rows), cols]
            gr = g_ref[pl.ds(r0, rows), cols]
            lb = lb_ref[:, cols]
            f = lb + (1.0 - lb) * jax.nn.sigmoid(fr)
            logf = jnp.log(f)
            kk = 1.0 - f
            q = qr * jax.nn.sigmoid(qr)
            gd = jnp.dot(cmat_ref[...], logf, precision=HIGHEST, preferred_element_type=F32)
            g = gd[0:rows]
            kdec = kk * jnp.exp(gd[rows:2 * rows])
            qdec = q * jnp.exp(g)
            vb = v.astype(BF16)
            a = eye_ref[...] * jnp.sum(q * kk, axis=1, keepdims=True)
            for l in range(n_levels):
                e = jnp.exp(gd[(l + 2) * rows:(l + 3) * rows])
                al = lax.dot_general((q * e).astype(BF16), (kk * e).astype(BF16), NT_DIMS,
                                     preferred_element_type=F32)
                a = a + al * mask_ref[l]
            o = jnp.dot(a.astype(BF16), vb, preferred_element_type=F32)
            o_inter = []
            for j in range(nseq):
                rs = slice(j * seq_len, (j + 1) * seq_len)
                s = s_scr[j * hb + hh]
                o_inter.append(jnp.dot(qdec[rs].astype(BF16), s.astype(BF16),
                                       preferred_element_type=F32))
                g_last = lax.dot_general(logf[rs], ones_blk, TN_DIMS, precision=HIGHEST,
                                         preferred_element_type=F32)
                upd = lax.dot_general(kdec[rs].astype(BF16), v[rs].astype(BF16), TN_DIMS,
                                      preferred_element_type=F32)
                s_scr[j * hb + hh] = jnp.exp(g_last) * s + upd
            o = o + (o_inter[0] if nseq == 1 else jnp.concatenate(o_inter, axis=0))
            on = o * lax.rsqrt(jnp.mean(o * o, axis=-1, keepdims=True) + EPS) * ng_ref[:, cols]
            o_ref[pl.ds(r0, rows), cols] = (on * (gr * jax.nn.sigmoid(gr))).astype(o_ref.dtype)
        return carry

    if n_chunks == 1:
        chunk(0, 0)
    else:
        lax.fori_loop(0, n_chunks, chunk, 0)

    @pl.when(t == pl.num_programs(2) - 1)
    def _():
        for j in range(nseq):
            for hh in range(hb):
                sfin_ref[j, hh] = s_scr[j * hb + hh]


def _hgrn_scan(z, lb, ng, *, row0, n_seq, seq_len, d_a, s0):
    dk = LANES
    n_heads = d_a // dk
    sec = d_a // dk
    if s0 is None:
        rows = math.gcd(seq_len, HGRN_CHUNK)
        nseq_blk = 1
        hb = 2
        tbk = min(seq_len, 512)
        n_t = seq_len // tbk
        grid = (n_seq, n_heads // hb, n_t)
        n_chunks = tbk // rows
        blk_rows = tbk
        row_blk = lambda b, h, t: row0 // tbk + b * n_t + t
        out_row_blk = lambda b, h, t: b * n_t + t
        consts = _scan_consts(rows, rows)
    else:
        rows = LANES
        nseq_blk = rows // seq_len
        hb = 1
        grid = (n_seq // nseq_blk, n_heads, 1)
        n_chunks = 1
        blk_rows = rows
        row_blk = lambda b, h, t: row0 // rows + b
        out_row_blk = lambda b, h, t: b
        consts = _scan_consts(rows, seq_len)
    assert row0 % blk_rows == 0
    cmat, masks, eye = (jnp.asarray(c) for c in consts)
    n_levels = masks.shape[0]
    w = hb * dk

    def zspec(section):
        return pl.BlockSpec((blk_rows, w), lambda b, h, t: (row_blk(b, h, t), section * (sec // hb) + h))

    vec_spec = pl.BlockSpec((1, w), lambda b, h, t: (0, h))
    full = lambda arr: pl.BlockSpec(arr.shape, lambda b, h, t: (0,) * arr.ndim)
    state_spec = pl.BlockSpec((nseq_blk, hb, dk, dk), lambda b, h, t: (b, h, 0, 0))
    in_specs = [zspec(0), zspec(1), zspec(2), zspec(3), vec_spec, vec_spec,
                full(cmat), full(masks), full(eye)]
    args = [z, z, z, z, lb.reshape(1, d_a), ng.reshape(1, d_a), cmat, masks, eye]
    if s0 is not None:
        in_specs.append(state_spec)
        args.append(s0)
    total = n_seq * seq_len
    kern = functools.partial(_scan_kernel, rows=rows, nseq=nseq_blk, n_chunks=n_chunks, hb=hb,
                             n_levels=n_levels, has_s0=s0 is not None)
    return pl.pallas_call(
        kern,
        grid=grid,
        in_specs=in_specs,
        out_specs=[pl.BlockSpec((blk_rows, w), lambda b, h, t: (out_row_blk(b, h, t), h)),
                   state_spec],
        out_shape=[jax.ShapeDtypeStruct((total, d_a), BF16),
                   jax.ShapeDtypeStruct((n_seq, n_heads, dk, dk), F32)],
        scratch_shapes=[pltpu.VMEM((nseq_blk * hb, dk, dk), F32)],
        compiler_params=_params("parallel", "parallel", "arbitrary"),
        name="hgrn_scan",
    )(*args)


HALO = 32


def _conv_kernel(*refs, tb, conv_w, has_buf, row_sub, lane_sub):
    if has_buf:
        a_ref, b_ref, w_ref, cb_ref, lg_ref, lbias_ref, buf_ref, o_ref, bufnew_ref, up, acc = refs
    else:
        a_ref, b_ref, w_ref, cb_ref, lg_ref, lbias_ref, o_ref, bufnew_ref, up, acc = refs
    t = pl.program_id(1)
    hist = conv_w - 1
    pad = HALO - hist
    d_b = a_ref.shape[1]

    @pl.when(t == 0)
    def _():
        up[0:HALO, :] = jnp.zeros((HALO, d_b), F32)
        if has_buf:
            up[pad:HALO, :] = buf_ref[0]

    @pl.when(t > 0)
    def _():
        up[0:HALO, :] = up[tb:tb + HALO, :]

    up[HALO:HALO + tb, :] = a_ref[...] * jax.nn.sigmoid(b_ref[...])

    for r0 in range(0, tb, row_sub):
        for c0 in range(0, d_b, lane_sub):
            cs = slice(c0, c0 + lane_sub)
            part = jnp.broadcast_to(cb_ref[:, cs], (row_sub, lane_sub))
            for j in range(conv_w):
                part = part + w_ref[j:j + 1, cs] * up[r0 + pad + j:r0 + pad + j + row_sub, cs]
            acc[r0:r0 + row_sub, cs] = part

    c = acc[...]
    mu = jnp.mean(c, axis=-1, keepdims=True)
    xc = c - mu
    y = xc * lax.rsqrt(jnp.mean(xc * xc, axis=-1, keepdims=True) + EPS) * lg_ref[...] + lbias_ref[...]
    o_ref[...] = (y * jax.nn.sigmoid(y)).astype(o_ref.dtype)

    @pl.when(t == pl.num_programs(1) - 1)
    def _():
        bufnew_ref[0] = up[tb + pad:tb + HALO, :]


def _conv_module(z, conv_w, conv_b, ln_g, ln_b, *, row0, n_seq, seq_len, col_blk_a, d_b, buf):
    width = conv_w.shape[0]
    hist = width - 1
    assert hist <= HALO
    tb = min(seq_len, 128)
    n_t = seq_len // tb
    assert row0 % tb == 0 and seq_len % tb == 0
    row_sub = min(tb, 32)
    rb = lambda b, t: row0 // tb + b * n_t + t
    vec = lambda: pl.BlockSpec((1, d_b), lambda b, t: (0, 0))
    buf_spec = pl.BlockSpec((1, hist, d_b), lambda b, t: (b, 0, 0))
    in_specs = [pl.BlockSpec((tb, d_b), lambda b, t: (rb(b, t), col_blk_a)),
                pl.BlockSpec((tb, d_b), lambda b, t: (rb(b, t), col_blk_a + 1)),
                pl.BlockSpec((width, d_b), lambda b, t: (0, 0)), vec(), vec(), vec()]
    args = [z, z, conv_w, conv_b.reshape(1, d_b), ln_g.reshape(1, d_b), ln_b.reshape(1, d_b)]
    if buf is not None:
        in_specs.append(buf_spec)
        args.append(buf)
    kern = functools.partial(_conv_kernel, tb=tb, conv_w=width, has_buf=buf is not None,
                             row_sub=row_sub, lane_sub=min(d_b, 512))
    return pl.pallas_call(
        kern,
        grid=(n_seq, n_t),
        in_specs=in_specs,
        out_specs=[pl.BlockSpec((tb, d_b), lambda b, t: (b * n_t + t, 0)), buf_spec],
        out_shape=[jax.ShapeDtypeStruct((n_seq * seq_len, d_b), BF16),
                   jax.ShapeDtypeStruct((n_seq, hist, d_b), F32)],
        scratch_shapes=[pltpu.VMEM((HALO + tb, d_b), F32), pltpu.VMEM((tb, d_b), F32)],
        compiler_params=_params("parallel", "arbitrary"),
        name="conv_module",
    )(*args)


def _top_desc(s, k):
    vals = []
    for _ in range(k):
        m = jnp.max(s, axis=0, keepdims=True)
        vals.append(m)
        s = jnp.where(s == m, -jnp.inf, s)
    return jnp.concatenate(vals, axis=0)


def _route_kernel(qh_ref, keys_ref, s1_ref, c1_ref, s2_ref, e2_ref, tau_ref, *, n_heads, half):
    taus = []
    for h in range(n_heads):
        sc = []
        for c in range(2):
            qblk = qh_ref[:, (2 * h + c) * half:(2 * h + c + 1) * half].astype(BF16)
            sc.append(lax.dot_general(keys_ref[h, c], qblk, NT_DIMS, preferred_element_type=F32))
        s1, s2 = sc
        v1 = _top_desc(s1, PEER_TOPK)
        v2 = _top_desc(s2, PEER_TOPK)
        cand = jnp.concatenate([v1[a:a + 1] + v2 for a in range(PEER_TOPK)], axis=0)
        top = _top_desc(cand, PEER_TOPK)
        zsum = jnp.sum(jnp.exp(top - top[0:1]), axis=0, keepdims=True)
        s1_ref[h] = s1
        s2_ref[h] = s2
        c1_ref[h] = jnp.exp(s1 - v1[0:1]) / zsum
        e2_ref[h] = jnp.exp(s2 - v2[0:1])
        taus.append(top[PEER_TOPK - 1:PEER_TOPK])
    tau_ref[...] = jnp.concatenate(taus, axis=0)


def _peer_route(qh, keys, *, tt=256):
    t = qh.shape[0]
    n_heads, _, n_keys, half = keys.shape
    big = jax.ShapeDtypeStruct((n_heads, n_keys, t), F32)
    big_spec = pl.BlockSpec((n_heads, n_keys, tt), lambda i: (0, 0, i))
    return pl.pallas_call(
        functools.partial(_route_kernel, n_heads=n_heads, half=half),
        grid=(t // tt,),
        in_specs=[pl.BlockSpec((tt, qh.shape[1]), lambda i: (i, 0)),
                  pl.BlockSpec(keys.shape, lambda i: (0, 0, 0, 0))],
        out_specs=[big_spec, big_spec, big_spec, big_spec,
                   pl.BlockSpec((n_heads, tt), lambda i: (0, i))],
        out_shape=[big, big, big, big, jax.ShapeDtypeStruct((n_heads, t), F32)],
        compiler_params=_params("parallel"),
        name="peer_route",
    )(qh, keys)


def _peer_act_kernel(x_ref, u_ref, s1_ref, c1_ref, s2_ref, e2_ref, tau_ref, w_ref, *, n_heads, n_keys):
    te = u_ref.shape[0]
    j = pl.program_id(1)
    hid = lax.dot_general(u_ref[...], x_ref[...], NT_DIMS, preferred_element_type=F32)
    act = 0.5 * hid * (1.0 + lax.erf(hid * (2.0 ** -0.5)))
    for ii in range(te // n_keys):
        i1 = j * (te // n_keys) + ii
        gate = jnp.zeros((n_keys, x_ref.shape[0]), F32)
        for h in range(n_heads):
            a = s1_ref[h, pl.ds(i1, 1), :]
            c = c1_ref[h, pl.ds(i1, 1), :]
            hit = (s2_ref[h] + a) >= tau_ref[h:h + 1, :]
            gate = gate + jnp.where(hit, e2_ref[h] * c, 0.0)
        w_ref[ii * n_keys:(ii + 1) * n_keys, :] = (gate * act[ii * n_keys:(ii + 1) * n_keys]).astype(w_ref.dtype)


def _peer_act(xn, u, s1, c1, s2, e2, tau, *, tt=512, te=512):
    t, d = xn.shape
    n_exp = u.shape[0]
    n_heads, n_keys, _ = s1.shape
    big_spec = pl.BlockSpec((n_heads, n_keys, tt), lambda i, j: (0, 0, i))
    return pl.pallas_call(
        functools.partial(_peer_act_kernel, n_heads=n_heads, n_keys=n_keys),
        grid=(t // tt, n_exp // te),
        in_specs=[pl.BlockSpec((tt, d), lambda i, j: (i, 0)),
                  pl.BlockSpec((te, d), lambda i, j: (j, 0)),
                  big_spec, big_spec, big_spec, big_spec,
                  pl.BlockSpec((n_heads, tt), lambda i, j: (0, i))],
        out_specs=pl.BlockSpec((te, tt), lambda i, j: (j, i)),
        out_shape=jax.ShapeDtypeStruct((n_exp, t), BF16),
        compiler_params=_params("parallel", "arbitrary"),
        name="peer_act",
    )(xn, u, s1, c1, s2, e2, tau)


def _peer_out_kernel(w_ref, v_ref, r_ref, o_ref):
    k = pl.program_id(1)

    @pl.when(k == 0)
    def _():
        o_ref[...] = r_ref[...]

    o_ref[...] += lax.dot_general(w_ref[...], v_ref[...], TN_DIMS, preferred_element_type=F32)


def _peer_out(wt, v, res, *, tt=512, tk=512):
    n_exp, t = wt.shape
    d = v.shape[1]
    return pl.pallas_call(
        _peer_out_kernel,
        grid=(t // tt, n_exp // tk),
        in_specs=[pl.BlockSpec((tk, tt), lambda i, k: (k, i)),
                  pl.BlockSpec((tk, d), lambda i, k: (k, 0)),
                  pl.BlockSpec((tt, d), lambda i, k: (i, 0))],
        out_specs=pl.BlockSpec((tt, d), lambda i, k: (i, 0)),
        out_shape=jax.ShapeDtypeStruct((t, d), F32),
        compiler_params=_params("parallel", "arbitrary"),
        name="peer_out",
    )(wt, v, res)


def _layer(x, p, n_p, l_p, n_s, l_s, s0_s, buf_s, lb, norm1_g, w_in, hgrn_norm_g, conv_w, conv_b,
           conv_ln_g, conv_ln_b, w_out, norm2_g, peer_wq, peer_keys, peer_u, peer_v, ple_norm_g,
           ple_wg, ple_wp):
    d_a = lb.shape[0]
    d_b = conv_w.shape[1]
    t_p = n_p * l_p
    hn = _rmsnorm(x, norm1_g, BF16)
    z = _matmul(hn, w_in.astype(BF16))
    oa_p, sp = _hgrn_scan(z, lb, hgrn_norm_g, row0=0, n_seq=n_p, seq_len=l_p, d_a=d_a, s0=None)
    oa_s, ss = _hgrn_scan(z, lb, hgrn_norm_g, row0=t_p, n_seq=n_s, seq_len=l_s, d_a=d_a, s0=s0_s)
    col_a = 4 * d_a // d_b
    ob_p, cp = _conv_module(z, conv_w, conv_b, conv_ln_g, conv_ln_b, row0=0, n_seq=n_p, seq_len=l_p,
                            col_blk_a=col_a, d_b=d_b, buf=None)
    ob_s, cs = _conv_module(z, conv_w, conv_b, conv_ln_g, conv_ln_b, row0=t_p, n_seq=n_s, seq_len=l_s,
                            col_blk_a=col_a, d_b=d_b, buf=buf_s)
    o_cat = jnp.concatenate([jnp.concatenate([oa_p, ob_p], axis=1),
                             jnp.concatenate([oa_s, ob_s], axis=1)], axis=0)
    x = _matmul(o_cat, w_out.astype(BF16), res=x, tn=512)
    hn2 = _rmsnorm(x, norm2_g, BF16)
    qh = _matmul(hn2, peer_wq.astype(BF16))
    s1, c1, s2, e2, tau = _peer_route(qh, peer_keys.astype(BF16))
    wt = _peer_act(hn2, peer_u.astype(BF16), s1, c1, s2, e2, tau)
    x = _peer_out(wt, peer_v.astype(BF16), x)
    hn3 = _rmsnorm(x, ple_norm_g, BF16)
    x = _matmul(hn3, ple_wg.astype(BF16), res=x, ple=(p.astype(BF16), ple_wp.astype(BF16)),
                tn=512)
    return x, sp, cp, ss, cs


def kernel(x_prompt, x_sample, state_hgrn, state_conv, p_prompt, p_sample, lb_logits, norm1_g, w_in,
           hgrn_norm_g, conv_w, conv_b, conv_ln_g, conv_ln_b, w_out, norm2_g, peer_wq, peer_keys,
           peer_u, peer_v, ple_norm_g, ple_wg, ple_wp, final_g):
    n_p, l_p, d = x_prompt.shape
    n_s, l_s, _ = x_sample.shape
    depth = w_in.shape[0]
    t_p, t_s = n_p * l_p, n_s * l_s
    lb_all = jnp.cumsum(jax.nn.softmax(lb_logits.astype(F32), axis=0), axis=0)
    x = jnp.concatenate([x_prompt.reshape(t_p, d), x_sample.reshape(t_s, d)], axis=0)
    sp_l, cp_l, ss_l, cs_l = [], [], [], []
    for i in range(depth):
        p = jnp.concatenate([p_prompt[i].reshape(t_p, -1), p_sample[i].reshape(t_s, -1)], axis=0)
        x, sp, cp, ss, cs = _layer(
            x, p, n_p, l_p, n_s, l_s, state_hgrn[i], state_conv[i], lb_all[i], norm1_g[i], w_in[i],
            hgrn_norm_g[i], conv_w[i], conv_b[i], conv_ln_g[i], conv_ln_b[i], w_out[i], norm2_g[i],
            peer_wq[i], peer_keys[i], peer_u[i], peer_v[i], ple_norm_g[i], ple_wg[i], ple_wp[i])
        sp_l.append(sp.astype(state_hgrn.dtype))
        cp_l.append(cp.astype(state_conv.dtype))
        ss_l.append(ss.astype(state_hgrn.dtype))
        cs_l.append(cs.astype(state_conv.dtype))
    y_prompt = _rmsnorm(x, final_g, x_prompt.dtype, row0=0, rows=t_p).reshape(n_p, l_p, d)
    y_sample = _rmsnorm(x, final_g, x_sample.dtype, row0=t_p, rows=t_s).reshape(n_s, l_s, d)
    return (y_prompt, y_sample, jnp.stack(sp_l), jnp.stack(cp_l), jnp.stack(ss_l), jnp.stack(cs_l))
```

```python
import functools
import math

import numpy as np
import jax
import jax.numpy as jnp
from jax import lax
from jax.experimental import pallas as pl
from jax.experimental.pallas import tpu as pltpu

EPS = 1e-6
HGRN_CHUNK = 128
PEER_TOPK = 16
LANES = 128
VMEM_LIMIT = 56 * 1024 * 1024
HIGHEST = lax.Precision.HIGHEST
F32 = jnp.float32
BF16 = jnp.bfloat16
NT_DIMS = (((1,), (1,)), ((), ()))
TN_DIMS = (((0,), (0,)), ((), ()))


def _params(*sem, flags=None):
    return pltpu.CompilerParams(dimension_semantics=sem, vmem_limit_bytes=VMEM_LIMIT, flags=flags)


def _rmsnorm_kernel(x_ref, g_ref, o_ref):
    x = x_ref[...]
    inv = lax.rsqrt(jnp.mean(x * x, axis=-1, keepdims=True) + EPS)
    o_ref[...] = (x * inv * g_ref[...]).astype(o_ref.dtype)


def _rmsnorm(x, g, out_dtype, *, row0=0, rows=None, tm=256):
    t, d = x.shape
    rows = t - row0 if rows is None else rows
    assert rows % tm == 0 and row0 % tm == 0
    off = row0 // tm
    return pl.pallas_call(
        _rmsnorm_kernel,
        grid=(rows // tm,),
        in_specs=[pl.BlockSpec((tm, d), lambda i: (i + off, 0)),
                  pl.BlockSpec((1, d), lambda i: (0, 0))],
        out_specs=pl.BlockSpec((tm, d), lambda i: (i, 0)),
        out_shape=jax.ShapeDtypeStruct((rows, d), out_dtype),
        compiler_params=_params("parallel"),
        name="rmsnorm",
    )(x, g.reshape(1, d))


def _mm_kernel(a_ref, w_ref, o_ref):
    o_ref[...] = jnp.dot(a_ref[...], w_ref[...], preferred_element_type=F32)


def _mm_res_kernel(a_ref, w_ref, r_ref, o_ref):
    o_ref[...] = r_ref[...] + jnp.dot(a_ref[...], w_ref[...], preferred_element_type=F32)


def _mm_ple_kernel(a_ref, w_ref, p_ref, wp_ref, r_ref, o_ref):
    gate = jax.nn.sigmoid(jnp.dot(a_ref[...], w_ref[...], preferred_element_type=F32))
    proj = jnp.dot(p_ref[...], wp_ref[...], preferred_element_type=F32)
    o_ref[...] = r_ref[...] + gate * proj


def _matmul(a, w, *, res=None, ple=None, tm=1024, tn=1024):
    m, k = a.shape
    n = w.shape[1]
    assert m % tm == 0 and n % tn == 0
    in_specs = [pl.BlockSpec((tm, k), lambda i, j: (i, 0)),
                pl.BlockSpec((k, tn), lambda i, j: (0, j))]
    args = [a, w]
    body = _mm_kernel
    if ple is not None:
        p, wp = ple
        in_specs += [pl.BlockSpec((tm, p.shape[1]), lambda i, j: (i, 0)),
                     pl.BlockSpec((p.shape[1], tn), lambda i, j: (0, j))]
        args += [p, wp]
        body = _mm_ple_kernel
    elif res is not None:
        body = _mm_res_kernel
    if res is not None:
        in_specs.append(pl.BlockSpec((tm, tn), lambda i, j: (i, j)))
        args.append(res)
    return pl.pallas_call(
        body,
        grid=(m // tm, n // tn),
        in_specs=in_specs,
        out_specs=pl.BlockSpec((tm, tn), lambda i, j: (i, j)),
        out_shape=jax.ShapeDtypeStruct((m, n), F32),
        compiler_params=_params("parallel", "parallel"),
        name="matmul",
    )(*args)


def _scan_consts(rows, seq_len):
    r = np.arange(rows)
    seq = r // seq_len
    same_seq = seq[:, None] == seq[None, :]
    le = r[None, :] <= r[:, None]
    mats = [(le & same_seq), ((r[None, :] > r[:, None]) & same_seq)]
    masks = []
    m = seq_len
    while m >= 2:
        half = m // 2
        pos = r % m
        ref = r - pos + half - 1
        upper = pos >= half
        u = r[None, :]
        mq = (u > ref[:, None]) & (u <= r[:, None])
        mk = (u > r[:, None]) & (u <= ref[:, None])
        mats.append(np.where(upper[:, None], mq, mk))
        same_blk = (r // m)[:, None] == (r // m)[None, :]
        masks.append(same_blk & upper[:, None] & (~upper)[None, :])
        m = half
    cmat = np.tile(np.concatenate(mats, axis=0), (1, 3)).astype(np.float32)
    masks = np.stack(masks).astype(np.float32)
    eye = np.eye(rows, dtype=np.float32)
    sel = np.tile(np.repeat(seq[:, None] == np.arange(rows // seq_len)[None, :], LANES, axis=1), (3, 1))
    return cmat, masks, eye, sel.astype(np.float32)


def _scan_kernel(*refs, rows, nseq, n_chunks, hb, n_levels, has_s0):
    if has_s0:
        (q_ref, f_ref, i_ref, g_ref, lb_ref, ng_ref, cmat_ref, mask_ref, eye_ref, sel_ref, s0_ref,
         o_ref, sfin_ref, s_scr) = refs
    else:
        (q_ref, f_ref, i_ref, g_ref, lb_ref, ng_ref, cmat_ref, mask_ref, eye_ref, sel_ref,
         o_ref, sfin_ref, s_scr) = refs
    t = pl.program_id(2)
    dk = LANES
    seq_len = rows // nseq

    @pl.when(t == 0)
    def _():
        if has_s0:
            for j in range(nseq):
                for hh in range(hb):
                    s_scr[j * hb + hh] = s0_ref[j, hh]
        else:
            s_scr[...] = jnp.zeros_like(s_scr)

    def chunk(ci, carry):
        r0 = pl.multiple_of(ci * rows, rows)
        for hh in range(hb):
            cols = slice(hh * dk, (hh + 1) * dk)
            qr = q_ref[pl.ds(r0, rows), cols]
            fr = f_ref[pl.ds(r0, rows), cols]
            v = i_ref[pl.ds(r0, rows), cols]
            gr = g_ref[pl.ds(r0, rows), cols]
            lb = lb_ref[:, cols]
            f = lb + (1.0 - lb) * jax.nn.sigmoid(fr)
            logf = jnp.log(f)
            kk = 1.0 - f
            q = qr * jax.nn.sigmoid(qr)
            hi = logf.astype(BF16)
            r1 = logf - hi.astype(F32)
            mid = r1.astype(BF16)
            lo = (r1 - mid.astype(F32)).astype(BF16)
            pieces = jnp.concatenate([hi, mid, lo], axis=0)
            gd = jnp.dot(cmat_ref[...], pieces, preferred_element_type=F32)
            g_last = lax.dot_general(pieces, sel_ref[...], TN_DIMS,
                                     preferred_element_type=F32)
            g = gd[0:rows]
            kdec = kk * jnp.exp(gd[rows:2 * rows])
            qdec = q * jnp.exp(g)
            vb = v.astype(BF16)
            a = eye_ref[...] * jnp.sum(q * kk, axis=1, keepdims=True)
            for l in range(n_levels):
                e = jnp.exp(gd[(l + 2) * rows:(l + 3) * rows])
                al = lax.dot_general((q * e).astype(BF16), (kk * e).astype(BF16), NT_DIMS,
                                     preferred_element_type=F32)
                a = a + al * mask_ref[l]
            o = jnp.dot(a.astype(BF16), vb, preferred_element_type=F32)
            o_inter = []
            for j in range(nseq):
                rs = slice(j * seq_len, (j + 1) * seq_len)
                s = s_scr[j * hb + hh]
                o_inter.append(jnp.dot(qdec[rs].astype(BF16), s.astype(BF16),
                                       preferred_element_type=F32))
                upd = lax.dot_general(kdec[rs].astype(BF16), v[rs].astype(BF16), TN_DIMS,
                                      preferred_element_type=F32)
                s_scr[j * hb + hh] = jnp.exp(g_last[:, j * dk:(j + 1) * dk]) * s + upd
            o = o + (o_inter[0] if nseq == 1 else jnp.concatenate(o_inter, axis=0))
            on = o * lax.rsqrt(jnp.mean(o * o, axis=-1, keepdims=True) + EPS) * ng_ref[:, cols]
            o_ref[pl.ds(r0, rows), cols] = (on * (gr * jax.nn.sigmoid(gr))).astype(o_ref.dtype)
        return carry

    if n_chunks == 1:
        chunk(0, 0)
    else:
        lax.fori_loop(0, n_chunks, chunk, 0)

    @pl.when(t == pl.num_programs(2) - 1)
    def _():
        for j in range(nseq):
            for hh in range(hb):
                sfin_ref[j, hh] = s_scr[j * hb + hh]


def _hgrn_scan(z, lb, ng, *, row0, n_seq, seq_len, d_a, s0):
    dk = LANES
    n_heads = d_a // dk
    sec = d_a // dk
    if s0 is None:
        rows = math.gcd(seq_len, HGRN_CHUNK)
        nseq_blk = 1
        hb = min(8, n_heads)
        tbk = min(seq_len, 512)
        n_t = seq_len // tbk
        grid = (n_seq, n_heads // hb, n_t)
        n_chunks = tbk // rows
        blk_rows = tbk
        row_blk = lambda b, h, t: row0 // tbk + b * n_t + t
        out_row_blk = lambda b, h, t: b * n_t + t
        consts = _scan_consts(rows, rows)
    else:
        rows = LANES
        nseq_blk = rows // seq_len
        hb = 1
        grid = (n_seq // nseq_blk, n_heads, 1)
        n_chunks = 1
        blk_rows = rows
        row_blk = lambda b, h, t: row0 // rows + b
        out_row_blk = lambda b, h, t: b
        consts = _scan_consts(rows, seq_len)
    assert row0 % blk_rows == 0
    cmat, masks, eye, sel = (jnp.asarray(c) for c in consts)
    cmat, sel = cmat.astype(BF16), sel.astype(BF16)
    n_levels = masks.shape[0]
    w = hb * dk

    def zspec(section):
        return pl.BlockSpec((blk_rows, w), lambda b, h, t: (row_blk(b, h, t), section * (sec // hb) + h))

    vec_spec = pl.BlockSpec((1, w), lambda b, h, t: (0, h))
    full = lambda arr: pl.BlockSpec(arr.shape, lambda b, h, t: (0,) * arr.ndim)
    state_spec = pl.BlockSpec((nseq_blk, hb, dk, dk), lambda b, h, t: (b, h, 0, 0))
    in_specs = [zspec(0), zspec(1), zspec(2), zspec(3), vec_spec, vec_spec,
                full(cmat), full(masks), full(eye), full(sel)]
    args = [z, z, z, z, lb.reshape(1, d_a), ng.reshape(1, d_a), cmat, masks, eye, sel]
    if s0 is not None:
        in_specs.append(state_spec)
        args.append(s0)
    total = n_seq * seq_len
    kern = functools.partial(_scan_kernel, rows=rows, nseq=nseq_blk, n_chunks=n_chunks, hb=hb,
                             n_levels=n_levels, has_s0=s0 is not None)
    return pl.pallas_call(
        kern,
        grid=grid,
        in_specs=in_specs,
        out_specs=[pl.BlockSpec((blk_rows, w), lambda b, h, t: (out_row_blk(b, h, t), h)),
                   state_spec],
        out_shape=[jax.ShapeDtypeStruct((total, d_a), BF16),
                   jax.ShapeDtypeStruct((n_seq, n_heads, dk, dk), F32)],
        scratch_shapes=[pltpu.VMEM((nseq_blk * hb, dk, dk), F32)],
        compiler_params=_params("parallel", "parallel", "arbitrary"),
        name="hgrn_scan",
    )(*args)


HALO = 32


def _conv_kernel(*refs, tb, conv_w, has_buf, row_sub, lane_sub):
    if has_buf:
        a_ref, b_ref, w_ref, cb_ref, lg_ref, lbias_ref, buf_ref, o_ref, bufnew_ref, up, acc = refs
    else:
        a_ref, b_ref, w_ref, cb_ref, lg_ref, lbias_ref, o_ref, bufnew_ref, up, acc = refs
    t = pl.program_id(1)
    hist = conv_w - 1
    pad = HALO - hist
    d_b = a_ref.shape[1]

    @pl.when(t == 0)
    def _():
        up[0:HALO, :] = jnp.zeros((HALO, d_b), F32)
        if has_buf:
            up[pad:HALO, :] = buf_ref[0]

    @pl.when(t > 0)
    def _():
        up[0:HALO, :] = up[tb:tb + HALO, :]

    up[HALO:HALO + tb, :] = a_ref[...] * jax.nn.sigmoid(b_ref[...])

    for r0 in range(0, tb, row_sub):
        for c0 in range(0, d_b, lane_sub):
            cs = slice(c0, c0 + lane_sub)
            part = jnp.broadcast_to(cb_ref[:, cs], (row_sub, lane_sub))
            for j in range(conv_w):
                part = part + w_ref[j:j + 1, cs] * up[r0 + pad + j:r0 + pad + j + row_sub, cs]
            acc[r0:r0 + row_sub, cs] = part

    c = acc[...]
    mu = jnp.mean(c, axis=-1, keepdims=True)
    xc = c - mu
    y = xc * lax.rsqrt(jnp.mean(xc * xc, axis=-1, keepdims=True) + EPS) * lg_ref[...] + lbias_ref[...]
    o_ref[...] = (y * jax.nn.sigmoid(y)).astype(o_ref.dtype)

    @pl.when(t == pl.num_programs(1) - 1)
    def _():
        bufnew_ref[0] = up[tb + pad:tb + HALO, :]


def _conv_module(z, conv_w, conv_b, ln_g, ln_b, *, row0, n_seq, seq_len, col_blk_a, d_b, buf):
    width = conv_w.shape[0]
    hist = width - 1
    assert hist <= HALO
    tb = min(seq_len, 128)
    n_t = seq_len // tb
    assert row0 % tb == 0 and seq_len % tb == 0
    row_sub = min(tb, 32)
    rb = lambda b, t: row0 // tb + b * n_t + t
    vec = lambda: pl.BlockSpec((1, d_b), lambda b, t: (0, 0))
    buf_spec = pl.BlockSpec((1, hist, d_b), lambda b, t: (b, 0, 0))
    in_specs = [pl.BlockSpec((tb, d_b), lambda b, t: (rb(b, t), col_blk_a)),
                pl.BlockSpec((tb, d_b), lambda b, t: (rb(b, t), col_blk_a + 1)),
                pl.BlockSpec((width, d_b), lambda b, t: (0, 0)), vec(), vec(), vec()]
    args = [z, z, conv_w, conv_b.reshape(1, d_b), ln_g.reshape(1, d_b), ln_b.reshape(1, d_b)]
    if buf is not None:
        in_specs.append(buf_spec)
        args.append(buf)
    kern = functools.partial(_conv_kernel, tb=tb, conv_w=width, has_buf=buf is not None,
                             row_sub=row_sub, lane_sub=min(d_b, 512))
    return pl.pallas_call(
        kern,
        grid=(n_seq, n_t),
        in_specs=in_specs,
        out_specs=[pl.BlockSpec((tb, d_b), lambda b, t: (b * n_t + t, 0)), buf_spec],
        out_shape=[jax.ShapeDtypeStruct((n_seq * seq_len, d_b), BF16),
                   jax.ShapeDtypeStruct((n_seq, hist, d_b), F32)],
        scratch_shapes=[pltpu.VMEM((HALO + tb, d_b), F32), pltpu.VMEM((tb, d_b), F32)],
        compiler_params=_params("parallel", "arbitrary"),
        name="conv_module",
    )(*args)


def _top_desc(s, k, with_rank=False):
    vals = []
    rank = jnp.full(s.shape, float(k), F32)
    for i in range(k):
        m = jnp.max(s, axis=0, keepdims=True)
        vals.append(m)
        top = s == m
        if with_rank:
            rank = jnp.where(top, float(i), rank)
        s = jnp.where(top, -jnp.inf, s)
    vals = jnp.concatenate(vals, axis=0)
    return (vals, rank) if with_rank else vals


def _route_kernel(qh_ref, keys_ref, n1_ref, c1_ref, r2_ref, e2_ref, *, n_heads, half):
    k = PEER_TOPK
    for h in range(n_heads):
        sc = []
        for c in range(2):
            qblk = qh_ref[:, (2 * h + c) * half:(2 * h + c + 1) * half].astype(BF16)
            sc.append(lax.dot_general(keys_ref[h, c], qblk, NT_DIMS, preferred_element_type=F32))
        s1, s2 = sc
        v1, rank1 = _top_desc(s1, k, with_rank=True)
        v2, rank2 = _top_desc(s2, k, with_rank=True)
        sums = [v1[a:a + 1] + v2 for a in range(k)]
        top = _top_desc(jnp.concatenate(sums, axis=0), k)
        tau = top[k - 1:k]
        zsum = jnp.sum(jnp.exp(top - top[0:1]), axis=0, keepdims=True)
        n1 = jnp.zeros_like(s1)
        for a in range(k):
            n_a = jnp.sum(jnp.where(sums[a] >= tau, 1.0, 0.0), axis=0, keepdims=True)
            n1 = jnp.where(rank1 == float(a), n_a, n1)
        n1_ref[h] = n1
        c1_ref[h] = jnp.exp(s1 - v1[0:1]) / zsum
        r2_ref[h] = rank2.astype(r2_ref.dtype)
        e2_ref[h] = jnp.exp(s2 - v2[0:1]).astype(e2_ref.dtype)


def _peer_route(qh, keys, *, tt=256):
    t = qh.shape[0]
    n_heads, _, n_keys, half = keys.shape
    spec = pl.BlockSpec((n_heads, n_keys, tt), lambda i: (0, 0, i))
    f32 = jax.ShapeDtypeStruct((n_heads, n_keys, t), F32)
    b16 = jax.ShapeDtypeStruct((n_heads, n_keys, t), BF16)
    return pl.pallas_call(
        functools.partial(_route_kernel, n_heads=n_heads, half=half),
        grid=(t // tt,),
        in_specs=[pl.BlockSpec((tt, qh.shape[1]), lambda i: (i, 0)),
                  pl.BlockSpec(keys.shape, lambda i: (0, 0, 0, 0))],
        out_specs=[spec, spec, spec, spec],
        out_shape=[f32, f32, b16, b16],
        compiler_params=_params("parallel"),
        name="peer_route",
    )(qh, keys)


GATE_ROWS = 32


def _peer_act_kernel(x_ref, u_ref, n1_ref, c1_ref, r2_ref, e2_ref, w_ref, nb_scr, cb_scr, *,
                     n_heads, n_keys):
    te = u_ref.shape[0]
    tt = x_ref.shape[0]
    j = pl.program_id(1)
    hid = lax.dot_general(u_ref[...], x_ref[...], NT_DIMS, preferred_element_type=F32)
    act = 0.5 * hid * (1.0 + lax.erf(hid * (2.0 ** -0.5)))
    for ii in range(te // n_keys):
        i1 = j * (te // n_keys) + ii
        for h in range(n_heads):
            nb_scr[h] = jnp.broadcast_to(n1_ref[h, pl.ds(i1, 1), :], (GATE_ROWS, tt)).astype(BF16)
            cb_scr[h] = jnp.broadcast_to(c1_ref[h, pl.ds(i1, 1), :], (GATE_ROWS, tt)).astype(BF16)
        for r0 in range(0, n_keys, GATE_ROWS):
            rs = slice(r0, r0 + GATE_ROWS)
            gate = None
            for h in range(n_heads):
                term = jnp.where(r2_ref[h, rs, :] < nb_scr[h], e2_ref[h, rs, :] * cb_scr[h],
                                 jnp.zeros((), BF16))
                gate = term if gate is None else gate + term
            rows = slice(ii * n_keys + r0, ii * n_keys + r0 + GATE_ROWS)
            w_ref[rows, :] = gate * act[rows].astype(BF16)


def _peer_act(xn, u, n1, c1, r2, e2, *, tt=512, te=512):
    t, d = xn.shape
    n_exp = u.shape[0]
    n_heads, n_keys, _ = n1.shape
    big_spec = pl.BlockSpec((n_heads, n_keys, tt), lambda i, j: (0, 0, i))
    return pl.pallas_call(
        functools.partial(_peer_act_kernel, n_heads=n_heads, n_keys=n_keys),
        grid=(t // tt, n_exp // te),
        in_specs=[pl.BlockSpec((tt, d), lambda i, j: (i, 0)),
                  pl.BlockSpec((te, d), lambda i, j: (j, 0)),
                  big_spec, big_spec, big_spec, big_spec],
        out_specs=pl.BlockSpec((te, tt), lambda i, j: (j, i)),
        out_shape=jax.ShapeDtypeStruct((n_exp, t), BF16),
        scratch_shapes=[pltpu.VMEM((n_heads, GATE_ROWS, tt), BF16),
                        pltpu.VMEM((n_heads, GATE_ROWS, tt), BF16)],
        compiler_params=_params("parallel", "arbitrary"),
        name="peer_act",
    )(xn, u, n1, c1, r2, e2)


def _peer_out_kernel(w_ref, v_ref, r_ref, o_ref):
    k = pl.program_id(1)

    @pl.when(k == 0)
    def _():
        o_ref[...] = r_ref[...]

    o_ref[...] += lax.dot_general(w_ref[...], v_ref[...], TN_DIMS, preferred_element_type=F32)


def _peer_out(wt, v, res, *, tt=512, tk=512):
    n_exp, t = wt.shape
    d = v.shape[1]
    return pl.pallas_call(
        _peer_out_kernel,
        grid=(t // tt, n_exp // tk),
        in_specs=[pl.BlockSpec((tk, tt), lambda i, k: (k, i)),
                  pl.BlockSpec((tk, d), lambda i, k: (k, 0)),
                  pl.BlockSpec((tt, d), lambda i, k: (i, 0))],
        out_specs=pl.BlockSpec((tt, d), lambda i, k: (i, 0)),
        out_shape=jax.ShapeDtypeStruct((t, d), F32),
        compiler_params=_params("parallel", "arbitrary"),
        name="peer_out",
    )(wt, v, res)


def _layer(x, p, n_p, l_p, n_s, l_s, s0_s, buf_s, lb, norm1_g, w_in, hgrn_norm_g, conv_w, conv_b,
           conv_ln_g, conv_ln_b, w_out, norm2_g, peer_wq, peer_keys, peer_u, peer_v, ple_norm_g,
           ple_wg, ple_wp):
    d_a = lb.shape[0]
    d_b = conv_w.shape[1]
    t_p = n_p * l_p
    hn = _rmsnorm(x, norm1_g, BF16)
    z = _matmul(hn, w_in.astype(BF16))
    oa_p, sp = _hgrn_scan(z, lb, hgrn_norm_g, row0=0, n_seq=n_p, seq_len=l_p, d_a=d_a, s0=None)
    oa_s, ss = _hgrn_scan(z, lb, hgrn_norm_g, row0=t_p, n_seq=n_s, seq_len=l_s, d_a=d_a, s0=s0_s)
    col_a = 4 * d_a // d_b
    ob_p, cp = _conv_module(z, conv_w, conv_b, conv_ln_g, conv_ln_b, row0=0, n_seq=n_p, seq_len=l_p,
                            col_blk_a=col_a, d_b=d_b, buf=None)
    ob_s, cs = _conv_module(z, conv_w, conv_b, conv_ln_g, conv_ln_b, row0=t_p, n_seq=n_s, seq_len=l_s,
                            col_blk_a=col_a, d_b=d_b, buf=buf_s)
    o_cat = jnp.concatenate([jnp.concatenate([oa_p, ob_p], axis=1),
                             jnp.concatenate([oa_s, ob_s], axis=1)], axis=0)
    x = _matmul(o_cat, w_out.astype(BF16), res=x, tn=512)
    hn2 = _rmsnorm(x, norm2_g, BF16)
    qh = _matmul(hn2, peer_wq.astype(BF16))
    n1, c1, r2, e2 = _peer_route(qh, peer_keys.astype(BF16))
    wt = _peer_act(hn2, peer_u.astype(BF16), n1, c1, r2, e2)
    x = _peer_out(wt, peer_v.astype(BF16), x)
    hn3 = _rmsnorm(x, ple_norm_g, BF16)
    x = _matmul(hn3, ple_wg.astype(BF16), res=x, ple=(p.astype(BF16), ple_wp.astype(BF16)),
                tn=512)
    return x, sp, cp, ss, cs


def kernel(x_prompt, x_sample, state_hgrn, state_conv, p_prompt, p_sample, lb_logits, norm1_g, w_in,
           hgrn_norm_g, conv_w, conv_b, conv_ln_g, conv_ln_b, w_out, norm2_g, peer_wq, peer_keys,
           peer_u, peer_v, ple_norm_g, ple_wg, ple_wp, final_g):
    n_p, l_p, d = x_prompt.shape
    n_s, l_s, _ = x_sample.shape
    depth = w_in.shape[0]
    t_p, t_s = n_p * l_p, n_s * l_s
    lb_all = jnp.cumsum(jax.nn.softmax(lb_logits.astype(F32), axis=0), axis=0)
    x = jnp.concatenate([x_prompt.reshape(t_p, d), x_sample.reshape(t_s, d)], axis=0)
    sp_l, cp_l, ss_l, cs_l = [], [], [], []
    for i in range(depth):
        p = jnp.concatenate([p_prompt[i].reshape(t_p, -1), p_sample[i].reshape(t_s, -1)], axis=0)
        x, sp, cp, ss, cs = _layer(
            x, p, n_p, l_p, n_s, l_s, state_hgrn[i], state_conv[i], lb_all[i], norm1_g[i], w_in[i],
            hgrn_norm_g[i], conv_w[i], conv_b[i], conv_ln_g[i], conv_ln_b[i], w_out[i], norm2_g[i],
            peer_wq[i], peer_keys[i], peer_u[i], peer_v[i], ple_norm_g[i], ple_wg[i], ple_wp[i])
        sp_l.append(sp.astype(state_hgrn.dtype))
        cp_l.append(cp.astype(state_conv.dtype))
        ss_l.append(ss.astype(state_hgrn.dtype))
        cs_l.append(cs.astype(state_conv.dtype))
    y_prompt = _rmsnorm(x, final_g, x_prompt.dtype, row0=0, rows=t_p).reshape(n_p, l_p, d)
    y_sample = _rmsnorm(x, final_g, x_sample.dtype, row0=t_p, rows=t_s).reshape(n_s, l_s, d)
    return (y_prompt, y_sample, jnp.stack(sp_l), jnp.stack(cp_l), jnp.stack(ss_l), jnp.stack(cs_l))
```

```python
import functools
import math

import numpy as np
import jax
import jax.numpy as jnp
from jax import lax
from jax.experimental import pallas as pl
from jax.experimental.pallas import tpu as pltpu

EPS = 1e-6
HGRN_CHUNK = 128
PEER_TOPK = 16
LANES = 128
SUBLANES = 8
VMEM_LIMIT = 56 * 1024 * 1024
F32 = jnp.float32
BF16 = jnp.bfloat16
NT_DIMS = (((1,), (1,)), ((), ()))
TN_DIMS = (((0,), (0,)), ((), ()))


def _params(*sem):
    return pltpu.CompilerParams(dimension_semantics=sem, vmem_limit_bytes=VMEM_LIMIT)


def _rmsnorm_kernel(x_ref, g_ref, o_ref):
    x = x_ref[...]
    inv = lax.rsqrt(jnp.mean(x * x, axis=-1, keepdims=True) + EPS)
    o_ref[...] = (x * inv * g_ref[...]).astype(o_ref.dtype)


def _rmsnorm(x, g, out_dtype, *, tm=256):
    t, d = x.shape
    assert t % tm == 0
    return pl.pallas_call(
        _rmsnorm_kernel,
        grid=(t // tm,),
        in_specs=[pl.BlockSpec((tm, d), lambda i: (i, 0)),
                  pl.BlockSpec((1, d), lambda i: (0, 0))],
        out_specs=pl.BlockSpec((tm, d), lambda i: (i, 0)),
        out_shape=jax.ShapeDtypeStruct((t, d), out_dtype),
        compiler_params=_params("parallel"),
        name="rmsnorm",
    )(x, g.reshape(1, d))


def _mm_kernel(*refs, n_lhs, has_ple, has_res):
    o_ref = refs[-1]
    acc = None
    for k in range(n_lhs):
        part = jnp.dot(refs[2 * k][...], refs[2 * k + 1][...], preferred_element_type=F32)
        acc = part if acc is None else acc + part
    nxt = 2 * n_lhs
    if has_ple:
        proj = jnp.dot(refs[nxt][...], refs[nxt + 1][...], preferred_element_type=F32)
        acc = jax.nn.sigmoid(acc) * proj
        nxt += 2
    if has_res:
        acc = refs[nxt][...] + acc
    o_ref[...] = acc


def _matmul(pairs, *, res=None, ple=None, tm=1024, tn=1024):
    m = pairs[0][0].shape[0]
    n = pairs[0][1].shape[1]
    assert m % tm == 0 and n % tn == 0
    in_specs, args = [], []
    for a, w in pairs + ([ple] if ple is not None else []):
        in_specs += [pl.BlockSpec((tm, a.shape[1]), lambda i, j: (i, 0)),
                     pl.BlockSpec((w.shape[0], tn), lambda i, j: (0, j))]
        args += [a, w]
    if res is not None:
        in_specs.append(pl.BlockSpec((tm, tn), lambda i, j: (i, j)))
        args.append(res)
    return pl.pallas_call(
        functools.partial(_mm_kernel, n_lhs=len(pairs), has_ple=ple is not None, has_res=res is not None),
        grid=(m // tm, n // tn),
        in_specs=in_specs,
        out_specs=pl.BlockSpec((tm, tn), lambda i, j: (i, j)),
        out_shape=jax.ShapeDtypeStruct((m, n), F32),
        compiler_params=_params("parallel", "parallel"),
        name="matmul",
    )(*args)


def _scan_consts(rows, seq_len):
    r = np.arange(rows)
    seq = r // seq_len
    same_seq = seq[:, None] == seq[None, :]
    le = r[None, :] <= r[:, None]
    mats = [(le & same_seq), ((r[None, :] > r[:, None]) & same_seq)]
    masks = []
    m = seq_len
    while m >= 2:
        half = m // 2
        pos = r % m
        ref = r - pos + half - 1
        upper = pos >= half
        u = r[None, :]
        mq = (u > ref[:, None]) & (u <= r[:, None])
        mk = (u > r[:, None]) & (u <= ref[:, None])
        if half < SUBLANES:
            mats.append(np.where(upper[:, None], mq, mk))
        same_blk = (r // m)[:, None] == (r // m)[None, :]
        masks.append(same_blk & upper[:, None] & (~upper)[None, :])
        m = half
    cmat = np.tile(np.concatenate(mats, axis=0), (1, 3)).astype(np.float32)
    masks = np.stack(masks).astype(np.float32)
    eye = np.eye(rows, dtype=np.float32)
    sel = np.tile(np.repeat(seq[:, None] == np.arange(rows // seq_len)[None, :], LANES, axis=1), (3, 1))
    return cmat, masks, eye, sel.astype(np.float32)


def _scan_kernel(*refs, rows, nseq, n_chunks, hb, n_levels, has_s0):
    if has_s0:
        (q_ref, f_ref, i_ref, g_ref, lb_ref, ng_ref, cmat_ref, mask_ref, eye_ref, sel_ref, s0_ref,
         o_ref, sfin_ref, s_scr, g_scr) = refs
    else:
        (q_ref, f_ref, i_ref, g_ref, lb_ref, ng_ref, cmat_ref, mask_ref, eye_ref, sel_ref,
         o_ref, sfin_ref, s_scr, g_scr) = refs
    t = pl.program_id(2)
    dk = LANES
    seq_len = rows // nseq
    n_coarse = sum(1 for l in range(n_levels) if (seq_len >> l) // 2 >= SUBLANES)

    @pl.when(t == 0)
    def _():
        if has_s0:
            for j in range(nseq):
                for hh in range(hb):
                    s_scr[j * hb + hh] = s0_ref[j, hh]
        else:
            s_scr[...] = jnp.zeros_like(s_scr)

    def chunk(ci, carry):
        r0 = pl.multiple_of(ci * rows, rows)
        for hh in range(hb):
            cols = slice(hh * dk, (hh + 1) * dk)
            qr = q_ref[pl.ds(r0, rows), cols]
            fr = f_ref[pl.ds(r0, rows), cols]
            v = i_ref[pl.ds(r0, rows), cols]
            gr = g_ref[pl.ds(r0, rows), cols]
            lb = lb_ref[:, cols]
            f = lb + (1.0 - lb) * jax.nn.sigmoid(fr)
            logf = jnp.log(f)
            kk = 1.0 - f
            q = qr * jax.nn.sigmoid(qr)
            hi = logf.astype(BF16)
            r1 = logf - hi.astype(F32)
            mid = r1.astype(BF16)
            lo = (r1 - mid.astype(F32)).astype(BF16)
            pieces = jnp.concatenate([hi, mid, lo], axis=0)
            gd = jnp.dot(cmat_ref[...], pieces, preferred_element_type=F32)
            g_last = lax.dot_general(pieces, sel_ref[...], TN_DIMS,
                                     preferred_element_type=F32)
            g = gd[0:rows]
            kdec = kk * jnp.exp(gd[rows:2 * rows])
            qdec = q * jnp.exp(g)
            vb = v.astype(BF16)
            a = eye_ref[...] * jnp.sum(q * kk, axis=1, keepdims=True)
            g_scr[hh] = g
            for l in range(n_levels):
                m = seq_len >> l
                half = m // 2
                if half >= SUBLANES:
                    parts = []
                    for r8 in range(0, rows, SUBLANES):
                        ref = r8 - r8 % m + half - 1
                        g_ref_row = g_scr[hh, ref:ref + 1, :]
                        tile = g[r8:r8 + SUBLANES]
                        parts.append(tile - g_ref_row if r8 % m >= half else g_ref_row - tile)
                    d = jnp.concatenate(parts, axis=0)
                else:
                    lf = l - n_coarse
                    d = gd[(lf + 2) * rows:(lf + 3) * rows]
                e = jnp.exp(d)
                al = lax.dot_general((q * e).astype(BF16), (kk * e).astype(BF16), NT_DIMS,
                                     preferred_element_type=F32)
                a = a + al * mask_ref[l]
            o = jnp.dot(a.astype(BF16), vb, preferred_element_type=F32)
            o_inter = []
            for j in range(nseq):
                rs = slice(j * seq_len, (j + 1) * seq_len)
                s = s_scr[j * hb + hh]
                o_inter.append(jnp.dot(qdec[rs].astype(BF16), s.astype(BF16),
                                       preferred_element_type=F32))
                upd = lax.dot_general(kdec[rs].astype(BF16), v[rs].astype(BF16), TN_DIMS,
                                      preferred_element_type=F32)
                s_scr[j * hb + hh] = jnp.exp(g_last[:, j * dk:(j + 1) * dk]) * s + upd
            o = o + (o_inter[0] if nseq == 1 else jnp.concatenate(o_inter, axis=0))
            on = o * lax.rsqrt(jnp.mean(o * o, axis=-1, keepdims=True) + EPS) * ng_ref[:, cols]
            o_ref[pl.ds(r0, rows), cols] = (on * (gr * jax.nn.sigmoid(gr))).astype(o_ref.dtype)
        return carry

    if n_chunks == 1:
        chunk(0, 0)
    else:
        lax.fori_loop(0, n_chunks, chunk, 0)

    @pl.when(t == pl.num_programs(2) - 1)
    def _():
        for j in range(nseq):
            for hh in range(hb):
                sfin_ref[j, hh] = s_scr[j * hb + hh]


def _hgrn_scan(z, lb, ng, *, n_seq, seq_len, d_a, s0):
    dk = LANES
    n_heads = d_a // dk
    sec = d_a // dk
    if s0 is None:
        rows = math.gcd(seq_len, HGRN_CHUNK)
        nseq_blk = 1
        hb = min(8, n_heads)
        tbk = min(seq_len, 512)
        n_t = seq_len // tbk
        grid = (n_seq, n_heads // hb, n_t)
        n_chunks = tbk // rows
        blk_rows = tbk
        row_blk = lambda b, h, t: b * n_t + t
        consts = _scan_consts(rows, rows)
    else:
        rows = LANES
        nseq_blk = rows // seq_len
        hb = 1
        grid = (n_seq // nseq_blk, n_heads, 1)
        n_chunks = 1
        blk_rows = rows
        row_blk = lambda b, h, t: b
        consts = _scan_consts(rows, seq_len)
    cmat, masks, eye, sel = (jnp.asarray(c) for c in consts)
    cmat, sel = cmat.astype(BF16), sel.astype(BF16)
    n_levels = masks.shape[0]
    w = hb * dk

    def zspec(section):
        return pl.BlockSpec((blk_rows, w), lambda b, h, t: (row_blk(b, h, t), section * (sec // hb) + h))

    vec_spec = pl.BlockSpec((1, w), lambda b, h, t: (0, h))
    full = lambda arr: pl.BlockSpec(arr.shape, lambda b, h, t: (0,) * arr.ndim)
    state_spec = pl.BlockSpec((nseq_blk, hb, dk, dk), lambda b, h, t: (b, h, 0, 0))
    in_specs = [zspec(0), zspec(1), zspec(2), zspec(3), vec_spec, vec_spec,
                full(cmat), full(masks), full(eye), full(sel)]
    args = [z, z, z, z, lb.reshape(1, d_a), ng.reshape(1, d_a), cmat, masks, eye, sel]
    if s0 is not None:
        in_specs.append(state_spec)
        args.append(s0)
    total = n_seq * seq_len
    kern = functools.partial(_scan_kernel, rows=rows, nseq=nseq_blk, n_chunks=n_chunks, hb=hb,
                             n_levels=n_levels, has_s0=s0 is not None)
    return pl.pallas_call(
        kern,
        grid=grid,
        in_specs=in_specs,
        out_specs=[pl.BlockSpec((blk_rows, w), lambda b, h, t: (row_blk(b, h, t), h)),
                   state_spec],
        out_shape=[jax.ShapeDtypeStruct((total, d_a), BF16),
                   jax.ShapeDtypeStruct((n_seq, n_heads, dk, dk), F32)],
        scratch_shapes=[pltpu.VMEM((nseq_blk * hb, dk, dk), F32), pltpu.VMEM((hb, rows, dk), F32)],
        compiler_params=_params("parallel", "parallel", "arbitrary"),
        name="hgrn_scan",
    )(*args)


HALO = 32


def _conv_kernel(*refs, tb, conv_w, has_buf, row_sub, lane_sub):
    if has_buf:
        a_ref, b_ref, w_ref, cb_ref, lg_ref, lbias_ref, buf_ref, o_ref, bufnew_ref, up, acc = refs
    else:
        a_ref, b_ref, w_ref, cb_ref, lg_ref, lbias_ref, o_ref, bufnew_ref, up, acc = refs
    t = pl.program_id(1)
    hist = conv_w - 1
    pad = HALO - hist
    d_b = a_ref.shape[1]

    @pl.when(t == 0)
    def _():
        up[0:HALO, :] = jnp.zeros((HALO, d_b), F32)
        up[HALO + tb:HALO + tb + SUBLANES, :] = jnp.zeros((SUBLANES, d_b), F32)
        if has_buf:
            up[pad:HALO, :] = buf_ref[0]

    @pl.when(t > 0)
    def _():
        up[0:HALO, :] = up[tb:tb + HALO, :]

    up[HALO:HALO + tb, :] = a_ref[...] * jax.nn.sigmoid(b_ref[...])

    for r0 in range(0, tb, row_sub):
        for c0 in range(0, d_b, lane_sub):
            cs = slice(c0, c0 + lane_sub)
            y = jnp.broadcast_to(cb_ref[:, cs], (row_sub, lane_sub))
            for phase in range(SUBLANES):
                n_rows = row_sub + (SUBLANES if phase else 0)
                part = None
                for j in range(conv_w):
                    if (pad + j) % SUBLANES != phase:
                        continue
                    base = r0 + pad + j - phase
                    term = w_ref[j:j + 1, cs] * up[base:base + n_rows, cs]
                    part = term if part is None else part + term
                if part is not None:
                    y = y + part[phase:phase + row_sub]
            acc[r0:r0 + row_sub, cs] = y

    c = acc[...]
    mu = jnp.mean(c, axis=-1, keepdims=True)
    xc = c - mu
    y = xc * lax.rsqrt(jnp.mean(xc * xc, axis=-1, keepdims=True) + EPS) * lg_ref[...] + lbias_ref[...]
    o_ref[...] = (y * jax.nn.sigmoid(y)).astype(o_ref.dtype)

    @pl.when(t == pl.num_programs(1) - 1)
    def _():
        bufnew_ref[0] = up[tb + pad:tb + HALO, :]


def _conv_module(z, conv_w, conv_b, ln_g, ln_b, *, n_seq, seq_len, col_blk_a, d_b, buf):
    width = conv_w.shape[0]
    hist = width - 1
    assert hist <= HALO
    tb = min(seq_len, 128)
    n_t = seq_len // tb
    assert seq_len % tb == 0
    row_sub = min(tb, 64)
    rb = lambda b, t: b * n_t + t
    vec = lambda: pl.BlockSpec((1, d_b), lambda b, t: (0, 0))
    buf_spec = pl.BlockSpec((1, hist, d_b), lambda b, t: (b, 0, 0))
    in_specs = [pl.BlockSpec((tb, d_b), lambda b, t: (rb(b, t), col_blk_a)),
                pl.BlockSpec((tb, d_b), lambda b, t: (rb(b, t), col_blk_a + 1)),
                pl.BlockSpec((width, d_b), lambda b, t: (0, 0)), vec(), vec(), vec()]
    args = [z, z, conv_w, conv_b.reshape(1, d_b), ln_g.reshape(1, d_b), ln_b.reshape(1, d_b)]
    if buf is not None:
        in_specs.append(buf_spec)
        args.append(buf)
    kern = functools.partial(_conv_kernel, tb=tb, conv_w=width, has_buf=buf is not None,
                             row_sub=row_sub, lane_sub=LANES)
    return pl.pallas_call(
        kern,
        grid=(n_seq, n_t),
        in_specs=in_specs,
        out_specs=[pl.BlockSpec((tb, d_b), lambda b, t: (rb(b, t), 0)), buf_spec],
        out_shape=[jax.ShapeDtypeStruct((n_seq * seq_len, d_b), BF16),
                   jax.ShapeDtypeStruct((n_seq, hist, d_b), F32)],
        scratch_shapes=[pltpu.VMEM((HALO + tb + SUBLANES, d_b), F32), pltpu.VMEM((tb, d_b), F32)],
        compiler_params=_params("parallel", "arbitrary"),
        name="conv_module",
    )(*args)


def _top_desc(s, k, with_rank=False):
    vals = []
    rank = jnp.full(s.shape, float(k), F32)
    for i in range(k):
        m = jnp.max(s, axis=0, keepdims=True)
        vals.append(m)
        top = s == m
        if with_rank:
            rank = jnp.where(top, float(i), rank)
        s = jnp.where(top, -jnp.inf, s)
    vals = jnp.concatenate(vals, axis=0)
    return (vals, rank) if with_rank else vals


def _route_kernel(qh_ref, keys_ref, n1_ref, c1_ref, r2_ref, e2_ref, *, n_heads, half):
    k = PEER_TOPK
    for h in range(n_heads):
        sc = []
        for c in range(2):
            qblk = qh_ref[:, (2 * h + c) * half:(2 * h + c + 1) * half].astype(BF16)
            sc.append(lax.dot_general(keys_ref[h, c], qblk, NT_DIMS, preferred_element_type=F32))
        s1, s2 = sc
        v1, rank1 = _top_desc(s1, k, with_rank=True)
        v2, rank2 = _top_desc(s2, k, with_rank=True)
        sums = [v1[a:a + 1] + v2 for a in range(k)]
        lim = 1
        while (lim + 1) ** 2 <= k:
            lim += 1
        rows = lax.broadcasted_iota(jnp.int32, v1.shape, 0)
        cand = sums[:lim] + [jnp.where(rows >= lim, v1 + v2[b:b + 1], -jnp.inf) for b in range(lim)]
        top = _top_desc(jnp.concatenate(cand, axis=0), k)
        tau = top[k - 1:k]
        zsum = jnp.sum(jnp.exp(top - top[0:1]), axis=0, keepdims=True)
        n1 = jnp.zeros_like(s1)
        for a in range(k):
            n_a = jnp.sum(jnp.where(sums[a] >= tau, 1.0, 0.0), axis=0, keepdims=True)
            n1 = jnp.where(rank1 == float(a), n_a, n1)
        n1_ref[h] = n1
        c1_ref[h] = jnp.exp(s1 - v1[0:1]) / zsum
        r2_ref[h] = rank2.astype(r2_ref.dtype)
        e2_ref[h] = jnp.exp(s2 - v2[0:1]).astype(e2_ref.dtype)


def _peer_route(qh, keys, *, tt=256):
    t = qh.shape[0]
    n_heads, _, n_keys, half = keys.shape
    spec = pl.BlockSpec((n_heads, n_keys, tt), lambda i: (0, 0, i))
    f32 = jax.ShapeDtypeStruct((n_heads, n_keys, t), F32)
    b16 = jax.ShapeDtypeStruct((n_heads, n_keys, t), BF16)
    return pl.pallas_call(
        functools.partial(_route_kernel, n_heads=n_heads, half=half),
        grid=(t // tt,),
        in_specs=[pl.BlockSpec((tt, qh.shape[1]), lambda i: (i, 0)),
                  pl.BlockSpec(keys.shape, lambda i: (0, 0, 0, 0))],
        out_specs=[spec, spec, spec, spec],
        out_shape=[f32, f32, b16, b16],
        compiler_params=_params("parallel"),
        name="peer_route",
    )(qh, keys)


GATE_ROWS = 32


def _peer_act_kernel(x_ref, u_ref, n1_ref, c1_ref, r2_ref, e2_ref, w_ref, nb_scr, cb_scr, *,
                     n_heads, n_keys):
    te = u_ref.shape[0]
    tt = x_ref.shape[0]
    j = pl.program_id(1)
    hid = lax.dot_general(u_ref[...], x_ref[...], NT_DIMS, preferred_element_type=F32)
    act = 0.5 * hid * (1.0 + lax.erf(hid * (2.0 ** -0.5)))
    for ii in range(te // n_keys):
        i1 = j * (te // n_keys) + ii
        for h in range(n_heads):
            nb_scr[h] = jnp.broadcast_to(n1_ref[h, pl.ds(i1, 1), :], (GATE_ROWS, tt)).astype(BF16)
            cb_scr[h] = jnp.broadcast_to(c1_ref[h, pl.ds(i1, 1), :], (GATE_ROWS, tt)).astype(BF16)
        for r0 in range(0, n_keys, GATE_ROWS):
            rs = slice(r0, r0 + GATE_ROWS)
            gate = None
            for h in range(n_heads):
                term = jnp.where(r2_ref[h, rs, :] < nb_scr[h], e2_ref[h, rs, :] * cb_scr[h],
                                 jnp.zeros((), BF16))
                gate = term if gate is None else gate + term
            rows = slice(ii * n_keys + r0, ii * n_keys + r0 + GATE_ROWS)
            w_ref[rows, :] = gate * act[rows].astype(BF16)


def _peer_act(xn, u, n1, c1, r2, e2, *, tt=512, te=512):
    t, d = xn.shape
    n_exp = u.shape[0]
    n_heads, n_keys, _ = n1.shape
    big_spec = pl.BlockSpec((n_heads, n_keys, tt), lambda i, j: (0, 0, i))
    return pl.pallas_call(
        functools.partial(_peer_act_kernel, n_heads=n_heads, n_keys=n_keys),
        grid=(t // tt, n_exp // te),
        in_specs=[pl.BlockSpec((tt, d), lambda i, j: (i, 0)),
                  pl.BlockSpec((te, d), lambda i, j: (j, 0)),
                  big_spec, big_spec, big_spec, big_spec],
        out_specs=pl.BlockSpec((te, tt), lambda i, j: (j, i)),
        out_shape=jax.ShapeDtypeStruct((n_exp, t), BF16),
        scratch_shapes=[pltpu.VMEM((n_heads, GATE_ROWS, tt), BF16),
                        pltpu.VMEM((n_heads, GATE_ROWS, tt), BF16)],
        compiler_params=_params("parallel", "arbitrary"),
        name="peer_act",
    )(xn, u, n1, c1, r2, e2)


def _peer_out_kernel(w_ref, v_ref, r_ref, g_ref, o_ref, n_ref):
    k = pl.program_id(1)

    @pl.when(k == 0)
    def _():
        o_ref[...] = r_ref[...]

    o_ref[...] += lax.dot_general(w_ref[...], v_ref[...], TN_DIMS, preferred_element_type=F32)

    @pl.when(k == pl.num_programs(1) - 1)
    def _():
        x = o_ref[...]
        inv = lax.rsqrt(jnp.mean(x * x, axis=-1, keepdims=True) + EPS)
        n_ref[...] = (x * inv * g_ref[...]).astype(n_ref.dtype)


def _peer_out(wt, v, res, norm_g, *, tt=512, tk=512):
    n_exp, t = wt.shape
    d = v.shape[1]
    row_spec = pl.BlockSpec((tt, d), lambda i, k: (i, 0))
    return pl.pallas_call(
        _peer_out_kernel,
        grid=(t // tt, n_exp // tk),
        in_specs=[pl.BlockSpec((tk, tt), lambda i, k: (k, i)),
                  pl.BlockSpec((tk, d), lambda i, k: (k, 0)),
                  pl.BlockSpec((tt, d), lambda i, k: (i, 0), pipeline_mode=pl.Buffered(1)),
                  pl.BlockSpec((1, d), lambda i, k: (0, 0))],
        out_specs=[row_spec, row_spec],
        out_shape=[jax.ShapeDtypeStruct((t, d), F32), jax.ShapeDtypeStruct((t, d), BF16)],
        compiler_params=_params("parallel", "arbitrary"),
        name="peer_out",
    )(wt, v, res, norm_g.reshape(1, d))


def _layer(x, p, n_seq, seq_len, s0, buf, lb, norm1_g, w_in, hgrn_norm_g, conv_w, conv_b, conv_ln_g,
           conv_ln_b, w_out_a, w_out_b, norm2_g, peer_wq, peer_keys, peer_u, peer_v, ple_norm_g, ple_wg,
           ple_wp):
    d_a = lb.shape[0]
    d_b = conv_w.shape[1]
    hn = _rmsnorm(x, norm1_g, BF16)
    z = _matmul([(hn, w_in)])
    o_a, s_new = _hgrn_scan(z, lb, hgrn_norm_g, n_seq=n_seq, seq_len=seq_len, d_a=d_a, s0=s0)
    o_b, buf_new = _conv_module(z, conv_w, conv_b, conv_ln_g, conv_ln_b, n_seq=n_seq, seq_len=seq_len,
                                col_blk_a=4 * d_a // d_b, d_b=d_b, buf=buf)
    x = _matmul([(o_a, w_out_a), (o_b, w_out_b)], res=x, tn=512)
    hn2 = _rmsnorm(x, norm2_g, BF16)
    qh = _matmul([(hn2, peer_wq)])
    n1, c1, r2, e2 = _peer_route(qh, peer_keys)
    wt = _peer_act(hn2, peer_u, n1, c1, r2, e2)
    x, hn3 = _peer_out(wt, peer_v, x, ple_norm_g)
    x = _matmul([(hn3, ple_wg)], res=x, ple=(p, ple_wp), tn=512)
    return x, s_new, buf_new


def kernel(x_prompt, x_sample, state_hgrn, state_conv, p_prompt, p_sample, lb_logits, norm1_g, w_in,
           hgrn_norm_g, conv_w, conv_b, conv_ln_g, conv_ln_b, w_out, norm2_g, peer_wq, peer_keys,
           peer_u, peer_v, ple_norm_g, ple_wg, ple_wp, final_g):
    n_p, l_p, d = x_prompt.shape
    n_s, l_s, _ = x_sample.shape
    depth = w_in.shape[0]
    d_a = lb_logits.shape[1]
    lb_all = jnp.cumsum(jax.nn.softmax(lb_logits.astype(F32), axis=0), axis=0)
    hp = x_prompt.reshape(n_p * l_p, d)
    hs = x_sample.reshape(n_s * l_s, d)
    sp_l, cp_l, ss_l, cs_l = [], [], [], []
    for i in range(depth):
        w = (lb_all[i], norm1_g[i], w_in[i].astype(BF16), hgrn_norm_g[i], conv_w[i], conv_b[i],
             conv_ln_g[i], conv_ln_b[i], w_out[i, :d_a].astype(BF16), w_out[i, d_a:].astype(BF16),
             norm2_g[i], peer_wq[i].astype(BF16), peer_keys[i].astype(BF16), peer_u[i].astype(BF16),
             peer_v[i].astype(BF16), ple_norm_g[i], ple_wg[i].astype(BF16), ple_wp[i].astype(BF16))
        hp, sp, cp = _layer(hp, p_prompt[i].reshape(n_p * l_p, -1).astype(BF16), n_p, l_p, None, None, *w)
        hs, ss, cs = _layer(hs, p_sample[i].reshape(n_s * l_s, -1).astype(BF16), n_s, l_s,
                            state_hgrn[i], state_conv[i], *w)
        sp_l.append(sp.astype(state_hgrn.dtype))
        cp_l.append(cp.astype(state_conv.dtype))
        ss_l.append(ss.astype(state_hgrn.dtype))
        cs_l.append(cs.astype(state_conv.dtype))
    y_prompt = _rmsnorm(hp, final_g, x_prompt.dtype).reshape(n_p, l_p, d)
    y_sample = _rmsnorm(hs, final_g, x_sample.dtype).reshape(n_s, l_s, d)
    return (y_prompt, y_sample, jnp.stack(sp_l), jnp.stack(cp_l), jnp.stack(ss_l), jnp.stack(cs_l))
```

```python
import functools
import math

import numpy as np
import jax
import jax.numpy as jnp
from jax import lax
from jax.experimental import pallas as pl
from jax.experimental.pallas import tpu as pltpu

EPS = 1e-6
HGRN_CHUNK = 128
PEER_TOPK = 16
LANES = 128
SUBLANES = 8
VMEM_LIMIT = 56 * 1024 * 1024
F32 = jnp.float32
BF16 = jnp.bfloat16
NT_DIMS = (((1,), (1,)), ((), ()))
TN_DIMS = (((0,), (0,)), ((), ()))


def _params(*sem):
    return pltpu.CompilerParams(dimension_semantics=sem, vmem_limit_bytes=VMEM_LIMIT)


def _rmsnorm_kernel(x_ref, g_ref, o_ref):
    x = x_ref[...]
    inv = lax.rsqrt(jnp.mean(x * x, axis=-1, keepdims=True) + EPS)
    o_ref[...] = (x * inv * g_ref[...]).astype(o_ref.dtype)


def _rmsnorm(x, g, out_dtype, *, tm=256):
    t, d = x.shape
    assert t % tm == 0
    return pl.pallas_call(
        _rmsnorm_kernel,
        grid=(t // tm,),
        in_specs=[pl.BlockSpec((tm, d), lambda i: (i, 0)),
                  pl.BlockSpec((1, d), lambda i: (0, 0))],
        out_specs=pl.BlockSpec((tm, d), lambda i: (i, 0)),
        out_shape=jax.ShapeDtypeStruct((t, d), out_dtype),
        compiler_params=_params("parallel"),
        name="rmsnorm",
    )(x, g.reshape(1, d))


def _mm_kernel(*refs, n_lhs, has_ple, has_res, cast_w):
    n_w = n_lhs + has_ple
    n_in = 2 * n_w + has_res
    o_ref = refs[n_in]
    if cast_w:
        w_refs = refs[n_in + 1:]

        @pl.when(pl.program_id(1) == 0)
        def _():
            for k in range(n_w):
                w_refs[k][...] = refs[2 * k + 1][...].astype(BF16)
    else:
        w_refs = [refs[2 * k + 1] for k in range(n_w)]
    acc = None
    for k in range(n_lhs):
        part = jnp.dot(refs[2 * k][...], w_refs[k][...], preferred_element_type=F32)
        acc = part if acc is None else acc + part
    if has_ple:
        proj = jnp.dot(refs[2 * n_lhs][...], w_refs[n_lhs][...], preferred_element_type=F32)
        acc = jax.nn.sigmoid(acc) * proj
    if has_res:
        acc = refs[n_in - 1][...] + acc
    o_ref[...] = acc


def _matmul(pairs, *, res=None, ple=None, tm=1024, tn=512):
    ops = pairs + ([ple] if ple is not None else [])
    m = ops[0][0].shape[0]
    n = ops[0][1][0].shape[1]
    cast_w = ops[0][1][0].dtype != BF16
    assert m % tm == 0 and n % tn == 0
    in_specs, args = [], []
    for a, (w, rb) in ops:
        kk = a.shape[1]
        in_specs += [pl.BlockSpec((tm, kk), lambda j, i: (i, 0)),
                     pl.BlockSpec((kk, tn), lambda j, i, rb=rb: (rb, j))]
        args += [a, w]
    if res is not None:
        in_specs.append(pl.BlockSpec((tm, tn), lambda j, i: (i, j)))
        args.append(res)
    out_specs = [pl.BlockSpec((tm, tn), lambda j, i: (i, j))]
    out_shape = [jax.ShapeDtypeStruct((m, n), F32)]
    if cast_w:
        for a, _ in ops:
            out_specs.append(pl.BlockSpec((a.shape[1], tn), lambda j, i: (0, j)))
            out_shape.append(jax.ShapeDtypeStruct((a.shape[1], n), BF16))
    outs = pl.pallas_call(
        functools.partial(_mm_kernel, n_lhs=len(pairs), has_ple=ple is not None, has_res=res is not None,
                          cast_w=cast_w),
        grid=(n // tn, m // tm),
        in_specs=in_specs,
        out_specs=out_specs,
        out_shape=out_shape,
        compiler_params=_params("parallel", "arbitrary"),
        name="matmul",
    )(*args)
    return outs[0], ([(wb, 0) for wb in outs[1:]] if cast_w else [w for _, w in ops])


def _scan_consts(rows, seq_len):
    r = np.arange(rows)
    seq = r // seq_len
    same_seq = seq[:, None] == seq[None, :]
    le = r[None, :] <= r[:, None]
    mats = [(le & same_seq), ((r[None, :] > r[:, None]) & same_seq)]
    masks = []
    m = seq_len
    while m >= 2:
        half = m // 2
        pos = r % m
        ref = r - pos + half - 1
        upper = pos >= half
        u = r[None, :]
        mq = (u > ref[:, None]) & (u <= r[:, None])
        mk = (u > r[:, None]) & (u <= ref[:, None])
        if half < SUBLANES:
            mats.append(np.where(upper[:, None], mq, mk))
        same_blk = (r // m)[:, None] == (r // m)[None, :]
        masks.append(same_blk & upper[:, None] & (~upper)[None, :])
        m = half
    cmat = np.tile(np.concatenate(mats, axis=0), (1, 3)).astype(np.float32)
    masks = np.stack(masks).astype(np.float32)
    eye = np.eye(rows, dtype=np.float32)
    sel = np.tile(np.repeat(seq[:, None] == np.arange(rows // seq_len)[None, :], LANES, axis=1), (3, 1))
    return cmat, masks, eye, sel.astype(np.float32)


def _scan_kernel(*refs, rows, nseq, n_chunks, hb, n_levels, has_s0):
    if has_s0:
        (q_ref, f_ref, i_ref, g_ref, lb_ref, ng_ref, cmat_ref, mask_ref, eye_ref, sel_ref, s0_ref,
         o_ref, sfin_ref, s_scr, g_scr) = refs
    else:
        (q_ref, f_ref, i_ref, g_ref, lb_ref, ng_ref, cmat_ref, mask_ref, eye_ref, sel_ref,
         o_ref, sfin_ref, s_scr, g_scr) = refs
    t = pl.program_id(2)
    dk = LANES
    seq_len = rows // nseq
    n_coarse = sum(1 for l in range(n_levels) if (seq_len >> l) // 2 >= SUBLANES)

    @pl.when(t == 0)
    def _():
        if has_s0:
            for j in range(nseq):
                for hh in range(hb):
                    s_scr[j * hb + hh] = s0_ref[j, hh]
        else:
            s_scr[...] = jnp.zeros_like(s_scr)

    def chunk(ci, carry):
        r0 = pl.multiple_of(ci * rows, rows)
        for hh in range(hb):
            cols = slice(hh * dk, (hh + 1) * dk)
            qr = q_ref[pl.ds(r0, rows), cols]
            fr = f_ref[pl.ds(r0, rows), cols]
            v = i_ref[pl.ds(r0, rows), cols]
            gr = g_ref[pl.ds(r0, rows), cols]
            lb = lb_ref[:, cols]
            f = lb + (1.0 - lb) * jax.nn.sigmoid(fr)
            logf = jnp.log(f)
            kk = 1.0 - f
            q = qr * jax.nn.sigmoid(qr)
            hi = logf.astype(BF16)
            r1 = logf - hi.astype(F32)
            mid = r1.astype(BF16)
            lo = (r1 - mid.astype(F32)).astype(BF16)
            pieces = jnp.concatenate([hi, mid, lo], axis=0)
            gd = jnp.dot(cmat_ref[...], pieces, preferred_element_type=F32)
            g_last = lax.dot_general(pieces, sel_ref[...], TN_DIMS,
                                     preferred_element_type=F32)
            g = gd[0:rows]
            kdec = kk * jnp.exp(gd[rows:2 * rows])
            qdec = q * jnp.exp(g)
            vb = v.astype(BF16)
            a = eye_ref[...] * jnp.sum(q * kk, axis=1, keepdims=True)
            g_scr[hh] = g
            for l in range(n_levels):
                m = seq_len >> l
                half = m // 2
                if half >= SUBLANES:
                    parts = []
                    for r8 in range(0, rows, SUBLANES):
                        ref = r8 - r8 % m + half - 1
                        g_ref_row = g_scr[hh, ref:ref + 1, :]
                        tile = g[r8:r8 + SUBLANES]
                        parts.append(tile - g_ref_row if r8 % m >= half else g_ref_row - tile)
                    d = jnp.concatenate(parts, axis=0)
                else:
                    lf = l - n_coarse
                    d = gd[(lf + 2) * rows:(lf + 3) * rows]
                e = jnp.exp(d)
                al = lax.dot_general((q * e).astype(BF16), (kk * e).astype(BF16), NT_DIMS,
                                     preferred_element_type=F32)
                a = a + al * mask_ref[l]
            o = jnp.dot(a.astype(BF16), vb, preferred_element_type=F32)
            o_inter = []
            for j in range(nseq):
                rs = slice(j * seq_len, (j + 1) * seq_len)
                s = s_scr[j * hb + hh]
                o_inter.append(jnp.dot(qdec[rs].astype(BF16), s.astype(BF16),
                                       preferred_element_type=F32))
                upd = lax.dot_general(kdec[rs].astype(BF16), v[rs].astype(BF16), TN_DIMS,
                                      preferred_element_type=F32)
                s_scr[j * hb + hh] = jnp.exp(g_last[:, j * dk:(j + 1) * dk]) * s + upd
            o = o + (o_inter[0] if nseq == 1 else jnp.concatenate(o_inter, axis=0))
            on = o * lax.rsqrt(jnp.mean(o * o, axis=-1, keepdims=True) + EPS) * ng_ref[:, cols]
            o_ref[pl.ds(r0, rows), cols] = (on * (gr * jax.nn.sigmoid(gr))).astype(o_ref.dtype)
        return carry

    if n_chunks == 1:
        chunk(0, 0)
    else:
        lax.fori_loop(0, n_chunks, chunk, 0)

    @pl.when(t == pl.num_programs(2) - 1)
    def _():
        for j in range(nseq):
            for hh in range(hb):
                sfin_ref[j, hh] = s_scr[j * hb + hh]


def _hgrn_scan(z, lb, ng, *, n_seq, seq_len, d_a, s0):
    dk = LANES
    n_heads = d_a // dk
    sec = d_a // dk
    if s0 is None:
        rows = math.gcd(seq_len, HGRN_CHUNK)
        nseq_blk = 1
        hb = min(8, n_heads)
        tbk = min(seq_len, 512)
        n_t = seq_len // tbk
        grid = (n_seq, n_heads // hb, n_t)
        n_chunks = tbk // rows
        blk_rows = tbk
        row_blk = lambda b, h, t: b * n_t + t
        consts = _scan_consts(rows, rows)
    else:
        rows = LANES
        nseq_blk = rows // seq_len
        hb = 1
        grid = (n_seq // nseq_blk, n_heads, 1)
        n_chunks = 1
        blk_rows = rows
        row_blk = lambda b, h, t: b
        consts = _scan_consts(rows, seq_len)
    cmat, masks, eye, sel = (jnp.asarray(c) for c in consts)
    cmat, sel = cmat.astype(BF16), sel.astype(BF16)
    n_levels = masks.shape[0]
    w = hb * dk

    def zspec(section):
        return pl.BlockSpec((blk_rows, w), lambda b, h, t: (row_blk(b, h, t), section * (sec // hb) + h))

    vec_spec = pl.BlockSpec((1, w), lambda b, h, t: (0, h))
    full = lambda arr: pl.BlockSpec(arr.shape, lambda b, h, t: (0,) * arr.ndim)
    state_spec = pl.BlockSpec((nseq_blk, hb, dk, dk), lambda b, h, t: (b, h, 0, 0))
    in_specs = [zspec(0), zspec(1), zspec(2), zspec(3), vec_spec, vec_spec,
                full(cmat), full(masks), full(eye), full(sel)]
    args = [z, z, z, z, lb.reshape(1, d_a), ng.reshape(1, d_a), cmat, masks, eye, sel]
    if s0 is not None:
        in_specs.append(state_spec)
        args.append(s0)
    total = n_seq * seq_len
    kern = functools.partial(_scan_kernel, rows=rows, nseq=nseq_blk, n_chunks=n_chunks, hb=hb,
                             n_levels=n_levels, has_s0=s0 is not None)
    return pl.pallas_call(
        kern,
        grid=grid,
        in_specs=in_specs,
        out_specs=[pl.BlockSpec((blk_rows, w), lambda b, h, t: (row_blk(b, h, t), h)),
                   state_spec],
        out_shape=[jax.ShapeDtypeStruct((total, d_a), BF16),
                   jax.ShapeDtypeStruct((n_seq, n_heads, dk, dk), F32)],
        scratch_shapes=[pltpu.VMEM((nseq_blk * hb, dk, dk), F32), pltpu.VMEM((hb, rows, dk), F32)],
        compiler_params=_params("parallel", "parallel", "arbitrary"),
        name="hgrn_scan",
    )(*args)


HALO = 32


def _conv_kernel(*refs, tb, conv_w, has_buf, row_sub, lane_sub):
    if has_buf:
        a_ref, b_ref, w_ref, cb_ref, lg_ref, lbias_ref, buf_ref, o_ref, bufnew_ref, up, acc = refs
    else:
        a_ref, b_ref, w_ref, cb_ref, lg_ref, lbias_ref, o_ref, bufnew_ref, up, acc = refs
    t = pl.program_id(1)
    hist = conv_w - 1
    pad = HALO - hist
    d_b = a_ref.shape[1]

    @pl.when(t == 0)
    def _():
        up[0:HALO, :] = jnp.zeros((HALO, d_b), F32)
        up[HALO + tb:HALO + tb + SUBLANES, :] = jnp.zeros((SUBLANES, d_b), F32)
        if has_buf:
            up[pad:HALO, :] = buf_ref[0]

    @pl.when(t > 0)
    def _():
        up[0:HALO, :] = up[tb:tb + HALO, :]

    up[HALO:HALO + tb, :] = a_ref[...] * jax.nn.sigmoid(b_ref[...])

    for r0 in range(0, tb, row_sub):
        for c0 in range(0, d_b, lane_sub):
            cs = slice(c0, c0 + lane_sub)
            y = jnp.broadcast_to(cb_ref[:, cs], (row_sub, lane_sub))
            for phase in range(SUBLANES):
                n_rows = row_sub + (SUBLANES if phase else 0)
                part = None
                for j in range(conv_w):
                    if (pad + j) % SUBLANES != phase:
                        continue
                    base = r0 + pad + j - phase
                    term = w_ref[j:j + 1, cs] * up[base:base + n_rows, cs]
                    part = term if part is None else part + term
                if part is not None:
                    y = y + part[phase:phase + row_sub]
            acc[r0:r0 + row_sub, cs] = y

    c = acc[...]
    mu = jnp.mean(c, axis=-1, keepdims=True)
    xc = c - mu
    y = xc * lax.rsqrt(jnp.mean(xc * xc, axis=-1, keepdims=True) + EPS) * lg_ref[...] + lbias_ref[...]
    o_ref[...] = (y * jax.nn.sigmoid(y)).astype(o_ref.dtype)

    @pl.when(t == pl.num_programs(1) - 1)
    def _():
        bufnew_ref[0] = up[tb + pad:tb + HALO, :]


def _conv_module(z, conv_w, conv_b, ln_g, ln_b, *, n_seq, seq_len, col_blk_a, d_b, buf):
    width = conv_w.shape[0]
    hist = width - 1
    assert hist <= HALO
    tb = min(seq_len, 128)
    n_t = seq_len // tb
    assert seq_len % tb == 0
    row_sub = min(tb, 64)
    rb = lambda b, t: b * n_t + t
    vec = lambda: pl.BlockSpec((1, d_b), lambda b, t: (0, 0))
    buf_spec = pl.BlockSpec((1, hist, d_b), lambda b, t: (b, 0, 0))
    in_specs = [pl.BlockSpec((tb, d_b), lambda b, t: (rb(b, t), col_blk_a)),
                pl.BlockSpec((tb, d_b), lambda b, t: (rb(b, t), col_blk_a + 1)),
                pl.BlockSpec((width, d_b), lambda b, t: (0, 0)), vec(), vec(), vec()]
    args = [z, z, conv_w, conv_b.reshape(1, d_b), ln_g.reshape(1, d_b), ln_b.reshape(1, d_b)]
    if buf is not None:
        in_specs.append(buf_spec)
        args.append(buf)
    kern = functools.partial(_conv_kernel, tb=tb, conv_w=width, has_buf=buf is not None,
                             row_sub=row_sub, lane_sub=LANES)
    return pl.pallas_call(
        kern,
        grid=(n_seq, n_t),
        in_specs=in_specs,
        out_specs=[pl.BlockSpec((tb, d_b), lambda b, t: (rb(b, t), 0)), buf_spec],
        out_shape=[jax.ShapeDtypeStruct((n_seq * seq_len, d_b), BF16),
                   jax.ShapeDtypeStruct((n_seq, hist, d_b), F32)],
        scratch_shapes=[pltpu.VMEM((HALO + tb + SUBLANES, d_b), F32), pltpu.VMEM((tb, d_b), F32)],
        compiler_params=_params("parallel", "arbitrary"),
        name="conv_module",
    )(*args)


def _top_desc(s, k, with_rank=False):
    vals = []
    rank = jnp.full(s.shape, float(k), F32)
    for i in range(k):
        m = jnp.max(s, axis=0, keepdims=True)
        vals.append(m)
        top = s == m
        if with_rank:
            rank = jnp.where(top, float(i), rank)
        s = jnp.where(top, -jnp.inf, s)
    vals = jnp.concatenate(vals, axis=0)
    return (vals, rank) if with_rank else vals


def _route_kernel(qh_ref, keys_ref, n1_ref, c1_ref, r2_ref, e2_ref, *, n_heads, half):
    k = PEER_TOPK
    for h in range(n_heads):
        sc = []
        for c in range(2):
            qblk = qh_ref[:, (2 * h + c) * half:(2 * h + c + 1) * half].astype(BF16)
            sc.append(lax.dot_general(keys_ref[h, c], qblk, NT_DIMS, preferred_element_type=F32))
        s1, s2 = sc
        v1, rank1 = _top_desc(s1, k, with_rank=True)
        v2, rank2 = _top_desc(s2, k, with_rank=True)
        sums = [v1[a:a + 1] + v2 for a in range(k)]
        lim = 1
        while (lim + 1) ** 2 <= k:
            lim += 1
        rows = lax.broadcasted_iota(jnp.int32, v1.shape, 0)
        cand = sums[:lim] + [jnp.where(rows >= lim, v1 + v2[b:b + 1], -jnp.inf) for b in range(lim)]
        top = _top_desc(jnp.concatenate(cand, axis=0), k)
        tau = top[k - 1:k]
        zsum = jnp.sum(jnp.exp(top - top[0:1]), axis=0, keepdims=True)
        n1 = jnp.zeros_like(s1)
        for a in range(k):
            n_a = jnp.sum(jnp.where(sums[a] >= tau, 1.0, 0.0), axis=0, keepdims=True)
            n1 = jnp.where(rank1 == float(a), n_a, n1)
        n1_ref[h] = n1
        c1_ref[h] = jnp.exp(s1 - v1[0:1]) / zsum
        r2_ref[h] = rank2.astype(r2_ref.dtype)
        e2_ref[h] = jnp.exp(s2 - v2[0:1]).astype(e2_ref.dtype)


def _peer_route(qh, keys, *, tt=256):
    t = qh.shape[0]
    n_heads, _, n_keys, half = keys.shape
    spec = pl.BlockSpec((n_heads, n_keys, tt), lambda i: (0, 0, i))
    f32 = jax.ShapeDtypeStruct((n_heads, n_keys, t), F32)
    b16 = jax.ShapeDtypeStruct((n_heads, n_keys, t), BF16)
    return pl.pallas_call(
        functools.partial(_route_kernel, n_heads=n_heads, half=half),
        grid=(t // tt,),
        in_specs=[pl.BlockSpec((tt, qh.shape[1]), lambda i: (i, 0)),
                  pl.BlockSpec(keys.shape, lambda i: (0, 0, 0, 0))],
        out_specs=[spec, spec, spec, spec],
        out_shape=[f32, f32, b16, b16],
        compiler_params=_params("parallel"),
        name="peer_route",
    )(qh, keys)


GATE_ROWS = 32


def _peer_act_kernel(x_ref, u_ref, n1_ref, c1_ref, r2_ref, e2_ref, w_ref, *rest, n_heads, n_keys, cast_u):
    te = u_ref.shape[0]
    tt = x_ref.shape[0]
    j = pl.program_id(0)
    if cast_u:
        ub_ref, nb_scr, cb_scr = rest

        @pl.when(pl.program_id(1) == 0)
        def _():
            ub_ref[...] = u_ref[...].astype(BF16)
    else:
        nb_scr, cb_scr = rest
        ub_ref = u_ref
    hid = lax.dot_general(ub_ref[...], x_ref[...], NT_DIMS, preferred_element_type=F32)
    act = hid * (0.5 + 0.5 * lax.erf(hid * (2.0 ** -0.5)))
    per_tile = te // n_keys
    per_blk = n1_ref.shape[1] // per_tile
    for ii in range(per_tile):
        row = (j % per_blk) * per_tile + ii
        for h in range(n_heads):
            nb_scr[h] = jnp.broadcast_to(n1_ref[h, pl.ds(row, 1), :], (GATE_ROWS, tt)).astype(BF16)
            cb_scr[h] = jnp.broadcast_to(c1_ref[h, pl.ds(row, 1), :], (GATE_ROWS, tt)).astype(BF16)
        for r0 in range(0, n_keys, GATE_ROWS):
            rs = slice(r0, r0 + GATE_ROWS)
            gate = None
            for h in range(n_heads):
                term = jnp.where(r2_ref[h, rs, :] < nb_scr[h], e2_ref[h, rs, :] * cb_scr[h],
                                 jnp.zeros((), BF16))
                gate = term if gate is None else gate + term
            rows = slice(ii * n_keys + r0, ii * n_keys + r0 + GATE_ROWS)
            w_ref[rows, :] = gate * act[rows].astype(BF16)


def _peer_act(xn, u, n1, c1, r2, e2, *, tt=512, te=512):
    t, d = xn.shape
    n_exp = u.shape[0]
    n_heads, n_keys, _ = n1.shape
    cast_u = u.dtype != BF16
    per_tile = te // n_keys
    blk_rows = max(SUBLANES, per_tile)
    per_blk = blk_rows // per_tile
    key_spec = pl.BlockSpec((n_heads, blk_rows, tt), lambda j, i: (0, j // per_blk, i))
    tok_spec = pl.BlockSpec((n_heads, n_keys, tt), lambda j, i: (0, 0, i))
    out_specs = [pl.BlockSpec((te, tt), lambda j, i: (j, i))]
    out_shape = [jax.ShapeDtypeStruct((n_exp, t), BF16)]
    if cast_u:
        out_specs.append(pl.BlockSpec((te, d), lambda j, i: (j, 0)))
        out_shape.append(jax.ShapeDtypeStruct((n_exp, d), BF16))
    outs = pl.pallas_call(
        functools.partial(_peer_act_kernel, n_heads=n_heads, n_keys=n_keys, cast_u=cast_u),
        grid=(n_exp // te, t // tt),
        in_specs=[pl.BlockSpec((tt, d), lambda j, i: (i, 0)),
                  pl.BlockSpec((te, d), lambda j, i: (j, 0)),
                  key_spec, key_spec, tok_spec, tok_spec],
        out_specs=out_specs,
        out_shape=out_shape,
        scratch_shapes=[pltpu.VMEM((n_heads, GATE_ROWS, tt), BF16),
                        pltpu.VMEM((n_heads, GATE_ROWS, tt), BF16)],
        compiler_params=_params("parallel", "arbitrary"),
        name="peer_act",
    )(xn, u, n1, c1, r2, e2)
    return outs[0], (outs[1] if cast_u else u)


def _peer_out_kernel(w_ref, v_ref, r_ref, g_ref, o_ref, n_ref):
    k = pl.program_id(1)

    @pl.when(k == 0)
    def _():
        o_ref[...] = r_ref[...]

    o_ref[...] += lax.dot_general(w_ref[...], v_ref[...], TN_DIMS, preferred_element_type=F32)

    @pl.when(k == pl.num_programs(1) - 1)
    def _():
        x = o_ref[...]
        inv = lax.rsqrt(jnp.mean(x * x, axis=-1, keepdims=True) + EPS)
        n_ref[...] = (x * inv * g_ref[...]).astype(n_ref.dtype)


def _peer_out(wt, v, res, norm_g, *, tt=512, tk=512):
    n_exp, t = wt.shape
    d = v.shape[1]
    row_spec = pl.BlockSpec((tt, d), lambda i, k: (i, 0))
    return pl.pallas_call(
        _peer_out_kernel,
        grid=(t // tt, n_exp // tk),
        in_specs=[pl.BlockSpec((tk, tt), lambda i, k: (k, i)),
                  pl.BlockSpec((tk, d), lambda i, k: (k, 0)),
                  pl.BlockSpec((tt, d), lambda i, k: (i, 0), pipeline_mode=pl.Buffered(1)),
                  pl.BlockSpec((1, d), lambda i, k: (0, 0))],
        out_specs=[row_spec, row_spec],
        out_shape=[jax.ShapeDtypeStruct((t, d), F32), jax.ShapeDtypeStruct((t, d), BF16)],
        compiler_params=_params("parallel", "arbitrary"),
        name="peer_out",
    )(wt, v, res, norm_g.reshape(1, d))


def _layer(x, p, n_seq, seq_len, s0, buf, w):
    d_a = w["lb"].shape[0]
    d_b = w["conv_w"].shape[1]
    hn = _rmsnorm(x, w["norm1_g"], BF16)
    z, (w_in,) = _matmul([(hn, w["w_in"])])
    o_a, s_new = _hgrn_scan(z, w["lb"], w["hgrn_norm_g"], n_seq=n_seq, seq_len=seq_len, d_a=d_a, s0=s0)
    o_b, buf_new = _conv_module(z, w["conv_w"], w["conv_b"], w["conv_ln_g"], w["conv_ln_b"], n_seq=n_seq,
                                seq_len=seq_len, col_blk_a=4 * d_a // d_b, d_b=d_b, buf=buf)
    x, (w_out_a, w_out_b) = _matmul([(o_a, w["w_out_a"]), (o_b, w["w_out_b"])], res=x)
    hn2 = _rmsnorm(x, w["norm2_g"], BF16)
    qh, (peer_wq,) = _matmul([(hn2, w["peer_wq"])])
    n1, c1, r2, e2 = _peer_route(qh, w["peer_keys"])
    wt, peer_u = _peer_act(hn2, w["peer_u"], n1, c1, r2, e2)
    x, hn3 = _peer_out(wt, w["peer_v"], x, w["ple_norm_g"])
    x, (ple_wg, ple_wp) = _matmul([(hn3, w["ple_wg"])], res=x, ple=(p, w["ple_wp"]))
    casted = dict(w_in=w_in, w_out_a=w_out_a, w_out_b=w_out_b, peer_wq=peer_wq, peer_u=peer_u,
                  ple_wg=ple_wg, ple_wp=ple_wp)
    return x, s_new, buf_new, casted


def kernel(x_prompt, x_sample, state_hgrn, state_conv, p_prompt, p_sample, lb_logits, norm1_g, w_in,
           hgrn_norm_g, conv_w, conv_b, conv_ln_g, conv_ln_b, w_out, norm2_g, peer_wq, peer_keys,
           peer_u, peer_v, ple_norm_g, ple_wg, ple_wp, final_g):
    n_p, l_p, d = x_prompt.shape
    n_s, l_s, _ = x_sample.shape
    depth = w_in.shape[0]
    lb_all = jnp.cumsum(jax.nn.softmax(lb_logits.astype(F32), axis=0), axis=0)
    hp = x_prompt.reshape(n_p * l_p, d)
    hs = x_sample.reshape(n_s * l_s, d)
    sp_l, cp_l, ss_l, cs_l = [], [], [], []
    for i in range(depth):
        w = dict(lb=lb_all[i], norm1_g=norm1_g[i], w_in=(w_in[i], 0), hgrn_norm_g=hgrn_norm_g[i],
                 conv_w=conv_w[i], conv_b=conv_b[i], conv_ln_g=conv_ln_g[i], conv_ln_b=conv_ln_b[i],
                 w_out_a=(w_out[i], 0), w_out_b=(w_out[i], 1), norm2_g=norm2_g[i], peer_wq=(peer_wq[i], 0),
                 peer_keys=peer_keys[i].astype(BF16), peer_u=peer_u[i], peer_v=peer_v[i].astype(BF16),
                 ple_norm_g=ple_norm_g[i], ple_wg=(ple_wg[i], 0), ple_wp=(ple_wp[i], 0))
        hp, sp, cp, casted = _layer(hp, p_prompt[i].reshape(n_p * l_p, -1).astype(BF16), n_p, l_p,
                                    None, None, w)
        hs, ss, cs, _ = _layer(hs, p_sample[i].reshape(n_s * l_s, -1).astype(BF16), n_s, l_s,
                               state_hgrn[i], state_conv[i], {**w, **casted})
        sp_l.append(sp.astype(state_hgrn.dtype))
        cp_l.append(cp.astype(state_conv.dtype))
        ss_l.append(ss.astype(state_hgrn.dtype))
        cs_l.append(cs.astype(state_conv.dtype))
    y_prompt = _rmsnorm(hp, final_g, x_prompt.dtype).reshape(n_p, l_p, d)
    y_sample = _rmsnorm(hs, final_g, x_sample.dtype).reshape(n_s, l_s, d)
    return (y_prompt, y_sample, jnp.stack(sp_l), jnp.stack(cp_l), jnp.stack(ss_l), jnp.stack(cs_l))
```

```python
import functools
import math

import numpy as np
import jax
import jax.numpy as jnp
from jax import lax
from jax.experimental import pallas as pl
from jax.experimental.pallas import tpu as pltpu

EPS = 1e-6
HGRN_CHUNK = 128
PEER_TOPK = 16
LANES = 128
SUBLANES = 8
VMEM_LIMIT = 56 * 1024 * 1024
F32 = jnp.float32
BF16 = jnp.bfloat16
NT_DIMS = (((1,), (1,)), ((), ()))
TN_DIMS = (((0,), (0,)), ((), ()))


def _params(*sem):
    return pltpu.CompilerParams(dimension_semantics=sem, vmem_limit_bytes=VMEM_LIMIT)


def _rmsnorm_kernel(x_ref, g_ref, o_ref):
    x = x_ref[...]
    inv = lax.rsqrt(jnp.mean(x * x, axis=-1, keepdims=True) + EPS)
    o_ref[...] = (x * inv * g_ref[...]).astype(o_ref.dtype)


def _rmsnorm(x, g, out_dtype, *, tm=256):
    t, d = x.shape
    assert t % tm == 0
    return pl.pallas_call(
        _rmsnorm_kernel,
        grid=(t // tm,),
        in_specs=[pl.BlockSpec((tm, d), lambda i: (i, 0)),
                  pl.BlockSpec((1, d), lambda i: (0, 0))],
        out_specs=pl.BlockSpec((tm, d), lambda i: (i, 0)),
        out_shape=jax.ShapeDtypeStruct((t, d), out_dtype),
        compiler_params=_params("parallel"),
        name="rmsnorm",
    )(x, g.reshape(1, d))


def _mm_kernel(*refs, n_lhs, has_ple, has_res):
    o_ref = refs[-1]
    acc = None
    for k in range(n_lhs):
        part = jnp.dot(refs[2 * k][...], refs[2 * k + 1][...], preferred_element_type=F32)
        acc = part if acc is None else acc + part
    nxt = 2 * n_lhs
    if has_ple:
        proj = jnp.dot(refs[nxt][...], refs[nxt + 1][...], preferred_element_type=F32)
        acc = jax.nn.sigmoid(acc) * proj
        nxt += 2
    if has_res:
        acc = refs[nxt][...] + acc
    o_ref[...] = acc


def _matmul(pairs, *, res=None, ple=None, tm=1024, tn=1024):
    m = pairs[0][0].shape[0]
    n = pairs[0][1].shape[1]
    assert m % tm == 0 and n % tn == 0
    in_specs, args = [], []
    for a, w in pairs + ([ple] if ple is not None else []):
        in_specs += [pl.BlockSpec((tm, a.shape[1]), lambda i, j: (i, 0)),
                     pl.BlockSpec((w.shape[0], tn), lambda i, j: (0, j))]
        args += [a, w]
    if res is not None:
        in_specs.append(pl.BlockSpec((tm, tn), lambda i, j: (i, j)))
        args.append(res)
    return pl.pallas_call(
        functools.partial(_mm_kernel, n_lhs=len(pairs), has_ple=ple is not None, has_res=res is not None),
        grid=(m // tm, n // tn),
        in_specs=in_specs,
        out_specs=pl.BlockSpec((tm, tn), lambda i, j: (i, j)),
        out_shape=jax.ShapeDtypeStruct((m, n), F32),
        compiler_params=_params("parallel", "parallel"),
        name="matmul",
    )(*args)


def _scan_consts(rows, seq_len):
    r = np.arange(rows)
    seq = r // seq_len
    cum = (r[None, :] <= r[:, None]) & (seq[:, None] == seq[None, :])
    masks = []
    m = seq_len
    while m >= 2:
        upper = r % m >= m // 2
        same_blk = (r // m)[:, None] == (r // m)[None, :]
        masks.append(same_blk & upper[:, None] & (~upper)[None, :])
        m //= 2
    cmat = np.tile(cum, (1, 3)).astype(np.float32)
    masks = np.stack(masks).astype(np.float32)
    eye = np.eye(rows, dtype=np.float32)
    sel = np.tile(np.repeat(seq[:, None] == np.arange(rows // seq_len)[None, :], LANES, axis=1), (3, 1))
    return cmat, masks, eye, sel.astype(np.float32)


def _scan_kernel(*refs, rows, nseq, n_chunks, hb, n_levels, has_s0):
    if has_s0:
        (q_ref, f_ref, i_ref, g_ref, lb_ref, ng_ref, cmat_ref, mask_ref, eye_ref, sel_ref, s0_ref,
         o_ref, sfin_ref, s_scr, g_scr) = refs
    else:
        (q_ref, f_ref, i_ref, g_ref, lb_ref, ng_ref, cmat_ref, mask_ref, eye_ref, sel_ref,
         o_ref, sfin_ref, s_scr, g_scr) = refs
    t = pl.program_id(2)
    dk = LANES
    seq_len = rows // nseq
    sub_row = lax.broadcasted_iota(jnp.int32, (SUBLANES, dk), 0)

    @pl.when(t == 0)
    def _():
        if has_s0:
            for j in range(nseq):
                for hh in range(hb):
                    s_scr[j * hb + hh] = s0_ref[j, hh]
        else:
            s_scr[...] = jnp.zeros_like(s_scr)

    def chunk(ci, carry):
        r0 = pl.multiple_of(ci * rows, rows)
        for hh in range(hb):
            cols = slice(hh * dk, (hh + 1) * dk)
            qr = q_ref[pl.ds(r0, rows), cols]
            fr = f_ref[pl.ds(r0, rows), cols]
            v = i_ref[pl.ds(r0, rows), cols]
            gr = g_ref[pl.ds(r0, rows), cols]
            lb = lb_ref[:, cols]
            f = lb + (1.0 - lb) * jax.nn.sigmoid(fr)
            logf = jnp.log(f)
            kk = 1.0 - f
            q = qr * jax.nn.sigmoid(qr)
            hi = logf.astype(BF16)
            r1 = logf - hi.astype(F32)
            mid = r1.astype(BF16)
            lo = (r1 - mid.astype(F32)).astype(BF16)
            pieces = jnp.concatenate([hi, mid, lo], axis=0)
            g = jnp.dot(cmat_ref[...], pieces, preferred_element_type=F32)
            g_last = lax.dot_general(pieces, sel_ref[...], TN_DIMS,
                                     preferred_element_type=F32)
            g_scr[hh] = g
            g_row = lambda i: g_scr[hh, i:i + 1, :]
            tiles = [g[r8:r8 + SUBLANES] for r8 in range(0, rows, SUBLANES)]
            to_end = [g_row((r8 // seq_len + 1) * seq_len - 1) - tiles[r8 // SUBLANES]
                      for r8 in range(0, rows, SUBLANES)]
            kdec = kk * jnp.exp(jnp.concatenate(to_end, axis=0))
            qdec = q * jnp.exp(g)
            vb = v.astype(BF16)
            a = eye_ref[...] * jnp.sum(q * kk, axis=1, keepdims=True)
            q_tiles = [q[r8:r8 + SUBLANES] for r8 in range(0, rows, SUBLANES)]
            k_tiles = [kk[r8:r8 + SUBLANES] for r8 in range(0, rows, SUBLANES)]
            for l in range(n_levels):
                m = seq_len >> l
                half = m // 2
                parts = []
                for r8 in range(0, rows, SUBLANES):
                    refs = sorted({r - r % m + half - 1 for r in range(r8, r8 + SUBLANES)})
                    g_ref_rows = g_row(refs[-1])
                    for ref in reversed(refs[:-1]):
                        g_ref_rows = jnp.where(sub_row <= ref + half - r8, g_row(ref), g_ref_rows)
                    ti = r8 // SUBLANES
                    if half >= SUBLANES:
                        upper = r8 % m >= half
                        d = tiles[ti] - g_ref_rows if upper else g_ref_rows - tiles[ti]
                        x = q_tiles[ti] if upper else k_tiles[ti]
                    else:
                        d = -jnp.abs(tiles[ti] - g_ref_rows)
                        x = jnp.where(sub_row % m >= half, q_tiles[ti], k_tiles[ti])
                    parts.append(x * jnp.exp(d))
                xe = jnp.concatenate(parts, axis=0).astype(BF16)
                al = lax.dot_general(xe, xe, NT_DIMS, preferred_element_type=F32)
                a = jnp.where(mask_ref[l] != 0.0, al, a)
            o = jnp.dot(a.astype(BF16), vb, preferred_element_type=F32)
            o_inter = []
            for j in range(nseq):
                rs = slice(j * seq_len, (j + 1) * seq_len)
                s = s_scr[j * hb + hh]
                o_inter.append(jnp.dot(qdec[rs].astype(BF16), s.astype(BF16),
                                       preferred_element_type=F32))
                upd = lax.dot_general(kdec[rs].astype(BF16), v[rs].astype(BF16), TN_DIMS,
                                      preferred_element_type=F32)
                s_scr[j * hb + hh] = jnp.exp(g_last[:, j * dk:(j + 1) * dk]) * s + upd
            o = o + (o_inter[0] if nseq == 1 else jnp.concatenate(o_inter, axis=0))
            on = o * lax.rsqrt(jnp.mean(o * o, axis=-1, keepdims=True) + EPS) * ng_ref[:, cols]
            o_ref[pl.ds(r0, rows), cols] = (on * (gr * jax.nn.sigmoid(gr))).astype(o_ref.dtype)
        return carry

    if n_chunks == 1:
        chunk(0, 0)
    else:
        lax.fori_loop(0, n_chunks, chunk, 0)

    @pl.when(t == pl.num_programs(2) - 1)
    def _():
        for j in range(nseq):
            for hh in range(hb):
                sfin_ref[j, hh] = s_scr[j * hb + hh]


def _hgrn_scan(z, lb, ng, *, n_seq, seq_len, d_a, s0):
    dk = LANES
    n_heads = d_a // dk
    sec = d_a // dk
    if s0 is None:
        rows = math.gcd(seq_len, HGRN_CHUNK)
        nseq_blk = 1
        hb = min(8, n_heads)
        tbk = min(seq_len, 512)
        n_t = seq_len // tbk
        grid = (n_seq, n_heads // hb, n_t)
        n_chunks = tbk // rows
        blk_rows = tbk
        row_blk = lambda b, h, t: b * n_t + t
        consts = _scan_consts(rows, rows)
    else:
        rows = LANES
        nseq_blk = rows // seq_len
        hb = min(2, n_heads)
        grid = (n_seq // nseq_blk, n_heads // hb, 1)
        n_chunks = 1
        blk_rows = rows
        row_blk = lambda b, h, t: b
        consts = _scan_consts(rows, seq_len)
    cmat, masks, eye, sel = (jnp.asarray(c) for c in consts)
    cmat, sel = cmat.astype(BF16), sel.astype(BF16)
    n_levels = masks.shape[0]
    w = hb * dk

    def zspec(section):
        return pl.BlockSpec((blk_rows, w), lambda b, h, t: (row_blk(b, h, t), section * (sec // hb) + h))

    vec_spec = pl.BlockSpec((1, w), lambda b, h, t: (0, h))
    full = lambda arr: pl.BlockSpec(arr.shape, lambda b, h, t: (0,) * arr.ndim)
    state_spec = pl.BlockSpec((nseq_blk, hb, dk, dk), lambda b, h, t: (b, h, 0, 0))
    in_specs = [zspec(0), zspec(1), zspec(2), zspec(3), vec_spec, vec_spec,
                full(cmat), full(masks), full(eye), full(sel)]
    args = [z, z, z, z, lb.reshape(1, d_a), ng.reshape(1, d_a), cmat, masks, eye, sel]
    if s0 is not None:
        in_specs.append(state_spec)
        args.append(s0)
    total = n_seq * seq_len
    kern = functools.partial(_scan_kernel, rows=rows, nseq=nseq_blk, n_chunks=n_chunks, hb=hb,
                             n_levels=n_levels, has_s0=s0 is not None)
    return pl.pallas_call(
        kern,
        grid=grid,
        in_specs=in_specs,
        out_specs=[pl.BlockSpec((blk_rows, w), lambda b, h, t: (row_blk(b, h, t), h)),
                   state_spec],
        out_shape=[jax.ShapeDtypeStruct((total, d_a), BF16),
                   jax.ShapeDtypeStruct((n_seq, n_heads, dk, dk), F32)],
        scratch_shapes=[pltpu.VMEM((nseq_blk * hb, dk, dk), F32), pltpu.VMEM((hb, rows, dk), F32)],
        compiler_params=_params("parallel", "parallel", "arbitrary"),
        name="hgrn_scan",
    )(*args)


HALO = 32


def _conv_kernel(*refs, tb, n_sub, conv_w, has_buf, row_sub, lane_sub):
    if has_buf:
        a_ref, b_ref, w_ref, cb_ref, lg_ref, lbias_ref, buf_ref, o_ref, bufnew_ref, up, acc = refs
    else:
        a_ref, b_ref, w_ref, cb_ref, lg_ref, lbias_ref, o_ref, bufnew_ref, up, acc = refs
    t = pl.program_id(1)
    hist = conv_w - 1
    pad = HALO - hist
    d_b = a_ref.shape[1]

    for sq in range(n_sub):
        rows = slice(sq * tb, (sq + 1) * tb)

        @pl.when(t == 0)
        def _():
            up[0:HALO, :] = jnp.zeros((HALO, d_b), F32)
            up[HALO + tb:HALO + tb + SUBLANES, :] = jnp.zeros((SUBLANES, d_b), F32)
            if has_buf:
                up[pad:HALO, :] = buf_ref[sq]

        @pl.when(t > 0)
        def _():
            up[0:HALO, :] = up[tb:tb + HALO, :]

        up[HALO:HALO + tb, :] = a_ref[rows, :] * jax.nn.sigmoid(b_ref[rows, :])

        for r0 in range(0, tb, row_sub):
            for c0 in range(0, d_b, lane_sub):
                cs = slice(c0, c0 + lane_sub)
                y = jnp.broadcast_to(cb_ref[:, cs], (row_sub, lane_sub))
                for phase in range(SUBLANES):
                    n_rows = row_sub + (SUBLANES if phase else 0)
                    part = None
                    for j in range(conv_w):
                        if (pad + j) % SUBLANES != phase:
                            continue
                        base = r0 + pad + j - phase
                        term = w_ref[j:j + 1, cs] * up[base:base + n_rows, cs]
                        part = term if part is None else part + term
                    if part is not None:
                        y = y + part[phase:phase + row_sub]
                acc[r0:r0 + row_sub, cs] = y

        c = acc[...]
        mu = jnp.mean(c, axis=-1, keepdims=True)
        xc = c - mu
        y = xc * lax.rsqrt(jnp.mean(xc * xc, axis=-1, keepdims=True) + EPS) * lg_ref[...] + lbias_ref[...]
        o_ref[rows, :] = (y * jax.nn.sigmoid(y)).astype(o_ref.dtype)

        @pl.when(t == pl.num_programs(1) - 1)
        def _():
            bufnew_ref[sq] = up[tb + pad:tb + HALO, :]


def _conv_module(z, conv_w, conv_b, ln_g, ln_b, *, n_seq, seq_len, col_blk_a, d_b, buf):
    width = conv_w.shape[0]
    hist = width - 1
    assert hist <= HALO
    tb = min(seq_len, 128)
    n_t = seq_len // tb
    assert seq_len % tb == 0
    n_sub = math.gcd(n_seq, LANES // tb) if n_t == 1 else 1
    row_sub = min(tb, 64)
    rb = lambda b, t: b * n_t + t
    vec = lambda: pl.BlockSpec((1, d_b), lambda b, t: (0, 0))
    buf_spec = pl.BlockSpec((n_sub, hist, d_b), lambda b, t: (b, 0, 0))
    in_specs = [pl.BlockSpec((n_sub * tb, d_b), lambda b, t: (rb(b, t), col_blk_a)),
                pl.BlockSpec((n_sub * tb, d_b), lambda b, t: (rb(b, t), col_blk_a + 1)),
                pl.BlockSpec((width, d_b), lambda b, t: (0, 0)), vec(), vec(), vec()]
    args = [z, z, conv_w, conv_b.reshape(1, d_b), ln_g.reshape(1, d_b), ln_b.reshape(1, d_b)]
    if buf is not None:
        in_specs.append(buf_spec)
        args.append(buf)
    kern = functools.partial(_conv_kernel, tb=tb, n_sub=n_sub, conv_w=width, has_buf=buf is not None,
                             row_sub=row_sub, lane_sub=LANES)
    return pl.pallas_call(
        kern,
        grid=(n_seq // n_sub, n_t),
        in_specs=in_specs,
        out_specs=[pl.BlockSpec((n_sub * tb, d_b), lambda b, t: (rb(b, t), 0)), buf_spec],
        out_shape=[jax.ShapeDtypeStruct((n_seq * seq_len, d_b), BF16),
                   jax.ShapeDtypeStruct((n_seq, hist, d_b), F32)],
        scratch_shapes=[pltpu.VMEM((HALO + tb + SUBLANES, d_b), F32), pltpu.VMEM((tb, d_b), F32)],
        compiler_params=_params("parallel", "arbitrary"),
        name="conv_module",
    )(*args)


def _top_desc(s, k, with_rank=False):
    vals = []
    rank = jnp.full(s.shape, float(k), F32)
    for i in range(k):
        m = jnp.max(s, axis=0, keepdims=True)
        vals.append(m)
        top = s == m
        if with_rank:
            rank = jnp.where(top, float(i), rank)
        s = jnp.where(top, -jnp.inf, s)
    vals = jnp.concatenate(vals, axis=0)
    return (vals, rank) if with_rank else vals


def _route_kernel(qh_ref, keys_ref, n1_ref, c1_ref, r2_ref, e2_ref, *, n_heads, half):
    k = PEER_TOPK
    for h in range(n_heads):
        sc = []
        for c in range(2):
            qblk = qh_ref[:, (2 * h + c) * half:(2 * h + c + 1) * half].astype(BF16)
            sc.append(lax.dot_general(keys_ref[h, c], qblk, NT_DIMS, preferred_element_type=F32))
        s1, s2 = sc
        v1, rank1 = _top_desc(s1, k, with_rank=True)
        v2, rank2 = _top_desc(s2, k, with_rank=True)
        sums = [v1[a:a + 1] + v2 for a in range(k)]
        lim = 1
        while (lim + 1) ** 2 <= k:
            lim += 1
        rows = lax.broadcasted_iota(jnp.int32, v1.shape, 0)
        cand = sums[:lim] + [jnp.where(rows >= lim, v1 + v2[b:b + 1], -jnp.inf) for b in range(lim)]
        top = _top_desc(jnp.concatenate(cand, axis=0), k)
        tau = top[k - 1:k]
        zsum = jnp.sum(jnp.exp(top - top[0:1]), axis=0, keepdims=True)
        n1 = jnp.zeros_like(s1)
        for a in range(k):
            n_a = jnp.sum(jnp.where(sums[a] >= tau, 1.0, 0.0), axis=0, keepdims=True)
            n1 = jnp.where(rank1 == float(a), n_a, n1)
        n1_ref[h] = n1
        c1_ref[h] = jnp.exp(s1 - v1[0:1]) / zsum
        r2_ref[h] = rank2.astype(r2_ref.dtype)
        e2_ref[h] = jnp.exp(s2 - v2[0:1]).astype(e2_ref.dtype)


def _peer_route(qh, keys, *, tt=256):
    t = qh.shape[0]
    n_heads, _, n_keys, half = keys.shape
    spec = pl.BlockSpec((n_heads, n_keys, tt), lambda i: (0, 0, i))
    f32 = jax.ShapeDtypeStruct((n_heads, n_keys, t), F32)
    b16 = jax.ShapeDtypeStruct((n_heads, n_keys, t), BF16)
    return pl.pallas_call(
        functools.partial(_route_kernel, n_heads=n_heads, half=half),
        grid=(t // tt,),
        in_specs=[pl.BlockSpec((tt, qh.shape[1]), lambda i: (i, 0)),
                  pl.BlockSpec(keys.shape, lambda i: (0, 0, 0, 0))],
        out_specs=[spec, spec, spec, spec],
        out_shape=[f32, f32, b16, b16],
        compiler_params=_params("parallel"),
        name="peer_route",
    )(qh, keys)


GATE_ROWS = 32


def _peer_act_kernel(x_ref, u_ref, n1_ref, c1_ref, r2_ref, e2_ref, w_ref, nb_scr, cb_scr, *,
                     n_heads, n_keys):
    te = u_ref.shape[0]
    tt = x_ref.shape[0]
    j = pl.program_id(1)
    hid = lax.dot_general(u_ref[...], x_ref[...], NT_DIMS, preferred_element_type=F32)
    act = hid * (0.5 + 0.5 * lax.erf(hid * (2.0 ** -0.5)))
    for ii in range(te // n_keys):
        i1 = j * (te // n_keys) + ii
        for h in range(n_heads):
            nb_scr[h] = jnp.broadcast_to(n1_ref[h, pl.ds(i1, 1), :], (GATE_ROWS, tt)).astype(BF16)
            cb_scr[h] = jnp.broadcast_to(c1_ref[h, pl.ds(i1, 1), :], (GATE_ROWS, tt)).astype(BF16)
        for r0 in range(0, n_keys, GATE_ROWS):
            rs = slice(r0, r0 + GATE_ROWS)
            gate = None
            for h in range(n_heads):
                term = jnp.where(r2_ref[h, rs, :] < nb_scr[h], e2_ref[h, rs, :] * cb_scr[h],
                                 jnp.zeros((), BF16))
                gate = term if gate is None else gate + term
            rows = slice(ii * n_keys + r0, ii * n_keys + r0 + GATE_ROWS)
            w_ref[rows, :] = gate * act[rows].astype(BF16)


def _peer_act(xn, u, n1, c1, r2, e2, *, tt=512, te=1024):
    t, d = xn.shape
    n_exp = u.shape[0]
    n_heads, n_keys, _ = n1.shape
    big_spec = pl.BlockSpec((n_heads, n_keys, tt), lambda i, j: (0, 0, i))
    return pl.pallas_call(
        functools.partial(_peer_act_kernel, n_heads=n_heads, n_keys=n_keys),
        grid=(t // tt, n_exp // te),
        in_specs=[pl.BlockSpec((tt, d), lambda i, j: (i, 0)),
                  pl.BlockSpec((te, d), lambda i, j: (j, 0)),
                  big_spec, big_spec, big_spec, big_spec],
        out_specs=pl.BlockSpec((te, tt), lambda i, j: (j, i)),
        out_shape=jax.ShapeDtypeStruct((n_exp, t), BF16),
        scratch_shapes=[pltpu.VMEM((n_heads, GATE_ROWS, tt), BF16),
                        pltpu.VMEM((n_heads, GATE_ROWS, tt), BF16)],
        compiler_params=_params("parallel", "arbitrary"),
        name="peer_act",
    )(xn, u, n1, c1, r2, e2)


def _peer_out_kernel(w_ref, v_ref, r_ref, g_ref, o_ref, n_ref):
    k = pl.program_id(1)

    @pl.when(k == 0)
    def _():
        o_ref[...] = r_ref[...]

    o_ref[...] += lax.dot_general(w_ref[...], v_ref[...], TN_DIMS, preferred_element_type=F32)

    @pl.when(k == pl.num_programs(1) - 1)
    def _():
        x = o_ref[...]
        inv = lax.rsqrt(jnp.mean(x * x, axis=-1, keepdims=True) + EPS)
        n_ref[...] = (x * inv * g_ref[...]).astype(n_ref.dtype)


def _peer_out(wt, v, res, norm_g, *, tt=512, tk=512):
    n_exp, t = wt.shape
    d = v.shape[1]
    row_spec = pl.BlockSpec((tt, d), lambda i, k: (i, 0))
    return pl.pallas_call(
        _peer_out_kernel,
        grid=(t // tt, n_exp // tk),
        in_specs=[pl.BlockSpec((tk, tt), lambda i, k: (k, i)),
                  pl.BlockSpec((tk, d), lambda i, k: (k, 0)),
                  pl.BlockSpec((tt, d), lambda i, k: (i, 0), pipeline_mode=pl.Buffered(1)),
                  pl.BlockSpec((1, d), lambda i, k: (0, 0))],
        out_specs=[row_spec, row_spec],
        out_shape=[jax.ShapeDtypeStruct((t, d), F32), jax.ShapeDtypeStruct((t, d), BF16)],
        compiler_params=_params("parallel", "arbitrary"),
        name="peer_out",
    )(wt, v, res, norm_g.reshape(1, d))


def _layer(x, p, n_seq, seq_len, s0, buf, lb, norm1_g, w_in, hgrn_norm_g, conv_w, conv_b, conv_ln_g,
           conv_ln_b, w_out_a, w_out_b, norm2_g, peer_wq, peer_keys, peer_u, peer_v, ple_norm_g, ple_wg,
           ple_wp):
    d_a = lb.shape[0]
    d_b = conv_w.shape[1]
    hn = _rmsnorm(x, norm1_g, BF16)
    z = _matmul([(hn, w_in)])
    o_a, s_new = _hgrn_scan(z, lb, hgrn_norm_g, n_seq=n_seq, seq_len=seq_len, d_a=d_a, s0=s0)
    o_b, buf_new = _conv_module(z, conv_w, conv_b, conv_ln_g, conv_ln_b, n_seq=n_seq, seq_len=seq_len,
                                col_blk_a=4 * d_a // d_b, d_b=d_b, buf=buf)
    x = _matmul([(o_a, w_out_a), (o_b, w_out_b)], res=x, tn=512)
    hn2 = _rmsnorm(x, norm2_g, BF16)
    qh = _matmul([(hn2, peer_wq)])
    n1, c1, r2, e2 = _peer_route(qh, peer_keys)
    wt = _peer_act(hn2, peer_u, n1, c1, r2, e2)
    x, hn3 = _peer_out(wt, peer_v, x, ple_norm_g)
    x = _matmul([(hn3, ple_wg)], res=x, ple=(p, ple_wp), tn=512)
    return x, s_new, buf_new


def kernel(x_prompt, x_sample, state_hgrn, state_conv, p_prompt, p_sample, lb_logits, norm1_g, w_in,
           hgrn_norm_g, conv_w, conv_b, conv_ln_g, conv_ln_b, w_out, norm2_g, peer_wq, peer_keys,
           peer_u, peer_v, ple_norm_g, ple_wg, ple_wp, final_g):
    n_p, l_p, d = x_prompt.shape
    n_s, l_s, _ = x_sample.shape
    depth = w_in.shape[0]
    d_a = lb_logits.shape[1]
    lb_all = jnp.cumsum(jax.nn.softmax(lb_logits.astype(F32), axis=0), axis=0)
    hp = x_prompt.reshape(n_p * l_p, d)
    hs = x_sample.reshape(n_s * l_s, d)
    sp_l, cp_l, ss_l, cs_l = [], [], [], []
    for i in range(depth):
        w = (lb_all[i], norm1_g[i], w_in[i].astype(BF16), hgrn_norm_g[i], conv_w[i], conv_b[i],
             conv_ln_g[i], conv_ln_b[i], w_out[i, :d_a].astype(BF16), w_out[i, d_a:].astype(BF16),
             norm2_g[i], peer_wq[i].astype(BF16), peer_keys[i].astype(BF16), peer_u[i].astype(BF16),
             peer_v[i].astype(BF16), ple_norm_g[i], ple_wg[i].astype(BF16), ple_wp[i].astype(BF16))
        hp, sp, cp = _layer(hp, p_prompt[i].reshape(n_p * l_p, -1).astype(BF16), n_p, l_p, None, None, *w)
        hs, ss, cs = _layer(hs, p_sample[i].reshape(n_s * l_s, -1).astype(BF16), n_s, l_s,
                            state_hgrn[i], state_conv[i], *w)
        sp_l.append(sp.astype(state_hgrn.dtype))
        cp_l.append(cp.astype(state_conv.dtype))
        ss_l.append(ss.astype(state_hgrn.dtype))
        cs_l.append(cs.astype(state_conv.dtype))
    y_prompt = _rmsnorm(hp, final_g, x_prompt.dtype).reshape(n_p, l_p, d)
    y_sample = _rmsnorm(hs, final_g, x_sample.dtype).reshape(n_s, l_s, d)
    return (y_prompt, y_sample, jnp.stack(sp_l), jnp.stack(cp_l), jnp.stack(ss_l), jnp.stack(cs_l))
```

```python
import functools
import math

import numpy as np
import jax
import jax.numpy as jnp
from jax import lax
from jax.experimental import pallas as pl
from jax.experimental.pallas import tpu as pltpu

EPS = 1e-6
HGRN_CHUNK = 128
PEER_TOPK = 16
LANES = 128
SUBLANES = 8
VMEM_LIMIT = 60 * 1024 * 1024
F32 = jnp.float32
BF16 = jnp.bfloat16
NT_DIMS = (((1,), (1,)), ((), ()))
TN_DIMS = (((0,), (0,)), ((), ()))


def _params(*sem):
    return pltpu.CompilerParams(dimension_semantics=sem, vmem_limit_bytes=VMEM_LIMIT)


def _rmsnorm_kernel(x_ref, g_ref, o_ref):
    x = x_ref[...]
    inv = lax.rsqrt(jnp.mean(x * x, axis=-1, keepdims=True) + EPS)
    o_ref[...] = (x * inv * g_ref[...]).astype(o_ref.dtype)


def _rmsnorm(x, g, out_dtype, *, tm=512):
    t, d = x.shape
    assert t % tm == 0
    return pl.pallas_call(
        _rmsnorm_kernel,
        grid=(t // tm,),
        in_specs=[pl.BlockSpec((tm, d), lambda i: (i, 0)),
                  pl.BlockSpec((1, d), lambda i: (0, 0))],
        out_specs=pl.BlockSpec((tm, d), lambda i: (i, 0)),
        out_shape=jax.ShapeDtypeStruct((t, d), out_dtype),
        compiler_params=_params("parallel"),
        name="rmsnorm",
    )(x, g.reshape(1, d))


def _mm_kernel(*refs, n_lhs, has_ple, has_res):
    o_ref = refs[-1]
    acc = None
    for k in range(n_lhs):
        part = jnp.dot(refs[2 * k][...], refs[2 * k + 1][...], preferred_element_type=F32)
        acc = part if acc is None else acc + part
    nxt = 2 * n_lhs
    if has_ple:
        proj = jnp.dot(refs[nxt][...], refs[nxt + 1][...], preferred_element_type=F32)
        acc = jax.nn.sigmoid(acc) * proj
        nxt += 2
    if has_res:
        acc = refs[nxt][...] + acc
    o_ref[...] = acc


def _matmul(pairs, *, res=None, ple=None, tm=1024, tn=1024):
    m = pairs[0][0].shape[0]
    n = pairs[0][1].shape[1]
    assert m % tm == 0 and n % tn == 0
    in_specs, args = [], []
    for a, w in pairs + ([ple] if ple is not None else []):
        in_specs += [pl.BlockSpec((tm, a.shape[1]), lambda i, j: (i, 0)),
                     pl.BlockSpec((w.shape[0], tn), lambda i, j: (0, j))]
        args += [a, w]
    if res is not None:
        in_specs.append(pl.BlockSpec((tm, tn), lambda i, j: (i, j)))
        args.append(res)
    return pl.pallas_call(
        functools.partial(_mm_kernel, n_lhs=len(pairs), has_ple=ple is not None, has_res=res is not None),
        grid=(m // tm, n // tn),
        in_specs=in_specs,
        out_specs=pl.BlockSpec((tm, tn), lambda i, j: (i, j)),
        out_shape=jax.ShapeDtypeStruct((m, n), F32),
        compiler_params=_params("parallel", "parallel"),
        name="matmul",
    )(*args)


def _scan_consts(rows, seq_len):
    r = np.arange(rows)
    seq = r // seq_len
    cum = (r[None, :] <= r[:, None]) & (seq[:, None] == seq[None, :])
    masks = []
    m = seq_len
    while m >= 2:
        upper = r % m >= m // 2
        same_blk = (r // m)[:, None] == (r // m)[None, :]
        masks.append(same_blk & upper[:, None] & (~upper)[None, :])
        m //= 2
    cmat = np.tile(cum, (1, 3)).astype(np.float32)
    masks = np.stack(masks).astype(np.float32)
    eye = np.eye(rows, dtype=np.float32)
    sel = np.tile(np.repeat(seq[:, None] == np.arange(rows // seq_len)[None, :], LANES, axis=1), (3, 1))
    return cmat, masks, eye, sel.astype(np.float32)


def _scan_kernel(*refs, rows, nseq, n_chunks, hb, n_levels, has_s0):
    if has_s0:
        (q_ref, f_ref, i_ref, g_ref, lb_ref, ng_ref, cmat_ref, mask_ref, eye_ref, sel_ref, s0_ref,
         o_ref, sfin_ref, s_scr, g_scr) = refs
    else:
        (q_ref, f_ref, i_ref, g_ref, lb_ref, ng_ref, cmat_ref, mask_ref, eye_ref, sel_ref,
         o_ref, sfin_ref, s_scr, g_scr) = refs
    t = pl.program_id(2)
    dk = LANES
    seq_len = rows // nseq
    sub_row = lax.broadcasted_iota(jnp.int32, (SUBLANES, dk), 0)

    @pl.when(t == 0)
    def _():
        if has_s0:
            for j in range(nseq):
                for hh in range(hb):
                    s_scr[j * hb + hh] = s0_ref[j, hh]
        else:
            s_scr[...] = jnp.zeros_like(s_scr)

    def chunk(ci, carry):
        r0 = pl.multiple_of(ci * rows, rows)
        for hh in range(hb):
            cols = slice(hh * dk, (hh + 1) * dk)
            qr = q_ref[pl.ds(r0, rows), cols]
            fr = f_ref[pl.ds(r0, rows), cols]
            v = i_ref[pl.ds(r0, rows), cols]
            gr = g_ref[pl.ds(r0, rows), cols]
            lb = lb_ref[:, cols]
            f = lb + (1.0 - lb) * jax.nn.sigmoid(fr)
            logf = jnp.log(f)
            kk = 1.0 - f
            q = qr * jax.nn.sigmoid(qr)
            hi = logf.astype(BF16)
            r1 = logf - hi.astype(F32)
            mid = r1.astype(BF16)
            lo = (r1 - mid.astype(F32)).astype(BF16)
            pieces = jnp.concatenate([hi, mid, lo], axis=0)
            g = jnp.dot(cmat_ref[...], pieces, preferred_element_type=F32)
            g_last = lax.dot_general(pieces, sel_ref[...], TN_DIMS,
                                     preferred_element_type=F32)
            g_scr[hh] = g
            g_row = lambda i: g_scr[hh, i:i + 1, :]
            tiles = [g[r8:r8 + SUBLANES] for r8 in range(0, rows, SUBLANES)]
            to_end = [g_row((r8 // seq_len + 1) * seq_len - 1) - tiles[r8 // SUBLANES]
                      for r8 in range(0, rows, SUBLANES)]
            kdec = kk * jnp.exp(jnp.concatenate(to_end, axis=0))
            qdec = q * jnp.exp(g)
            vb = v.astype(BF16)
            a = eye_ref[...] * jnp.sum(q * kk, axis=1, keepdims=True)
            q_tiles = [q[r8:r8 + SUBLANES] for r8 in range(0, rows, SUBLANES)]
            k_tiles = [kk[r8:r8 + SUBLANES] for r8 in range(0, rows, SUBLANES)]
            for l in range(n_levels):
                m = seq_len >> l
                half = m // 2
                parts = []
                for r8 in range(0, rows, SUBLANES):
                    refs = sorted({r - r % m + half - 1 for r in range(r8, r8 + SUBLANES)})
                    g_ref_rows = g_row(refs[-1])
                    for ref in reversed(refs[:-1]):
                        g_ref_rows = jnp.where(sub_row <= ref + half - r8, g_row(ref), g_ref_rows)
                    ti = r8 // SUBLANES
                    if half >= SUBLANES:
                        upper = r8 % m >= half
                        d = tiles[ti] - g_ref_rows if upper else g_ref_rows - tiles[ti]
                        x = q_tiles[ti] if upper else k_tiles[ti]
                    else:
                        d = -jnp.abs(tiles[ti] - g_ref_rows)
                        x = jnp.where(sub_row % m >= half, q_tiles[ti], k_tiles[ti])
                    parts.append(x * jnp.exp(d))
                xe = jnp.concatenate(parts, axis=0).astype(BF16)
                al = lax.dot_general(xe, xe, NT_DIMS, preferred_element_type=F32)
                a = jnp.where(mask_ref[l] != 0.0, al, a)
            o = jnp.dot(a.astype(BF16), vb, preferred_element_type=F32)
            o_inter = []
            for j in range(nseq):
                rs = slice(j * seq_len, (j + 1) * seq_len)
                s = s_scr[j * hb + hh]
                o_inter.append(jnp.dot(qdec[rs].astype(BF16), s.astype(BF16),
                                       preferred_element_type=F32))
                upd = lax.dot_general(kdec[rs].astype(BF16), v[rs].astype(BF16), TN_DIMS,
                                      preferred_element_type=F32)
                s_scr[j * hb + hh] = jnp.exp(g_last[:, j * dk:(j + 1) * dk]) * s + upd
            o = o + (o_inter[0] if nseq == 1 else jnp.concatenate(o_inter, axis=0))
            on = o * lax.rsqrt(jnp.mean(o * o, axis=-1, keepdims=True) + EPS) * ng_ref[:, cols]
            o_ref[pl.ds(r0, rows), cols] = (on * (gr * jax.nn.sigmoid(gr))).astype(o_ref.dtype)
        return carry

    if n_chunks == 1:
        chunk(0, 0)
    else:
        lax.fori_loop(0, n_chunks, chunk, 0)

    @pl.when(t == pl.num_programs(2) - 1)
    def _():
        for j in range(nseq):
            for hh in range(hb):
                sfin_ref[j, hh] = s_scr[j * hb + hh]


def _hgrn_scan(z, lb, ng, *, n_seq, seq_len, d_a, s0):
    dk = LANES
    n_heads = d_a // dk
    sec = d_a // dk
    if s0 is None:
        rows = math.gcd(seq_len, HGRN_CHUNK)
        nseq_blk = 1
        hb = min(8, n_heads)
        tbk = min(seq_len, 512)
        n_t = seq_len // tbk
        grid = (n_seq, n_heads // hb, n_t)
        n_chunks = tbk // rows
        blk_rows = tbk
        row_blk = lambda b, h, t: b * n_t + t
        consts = _scan_consts(rows, rows)
    else:
        rows = LANES
        nseq_blk = rows // seq_len
        hb = min(2, n_heads)
        grid = (n_seq // nseq_blk, n_heads // hb, 1)
        n_chunks = 1
        blk_rows = rows
        row_blk = lambda b, h, t: b
        consts = _scan_consts(rows, seq_len)
    cmat, masks, eye, sel = (jnp.asarray(c) for c in consts)
    cmat, sel = cmat.astype(BF16), sel.astype(BF16)
    n_levels = masks.shape[0]
    w = hb * dk

    def zspec(section):
        return pl.BlockSpec((blk_rows, w), lambda b, h, t: (row_blk(b, h, t), section * (sec // hb) + h))

    vec_spec = pl.BlockSpec((1, w), lambda b, h, t: (0, h))
    full = lambda arr: pl.BlockSpec(arr.shape, lambda b, h, t: (0,) * arr.ndim)
    state_spec = pl.BlockSpec((nseq_blk, hb, dk, dk), lambda b, h, t: (b, h, 0, 0))
    in_specs = [zspec(0), zspec(1), zspec(2), zspec(3), vec_spec, vec_spec,
                full(cmat), full(masks), full(eye), full(sel)]
    args = [z, z, z, z, lb.reshape(1, d_a), ng.reshape(1, d_a), cmat, masks, eye, sel]
    if s0 is not None:
        in_specs.append(state_spec)
        args.append(s0)
    total = n_seq * seq_len
    kern = functools.partial(_scan_kernel, rows=rows, nseq=nseq_blk, n_chunks=n_chunks, hb=hb,
                             n_levels=n_levels, has_s0=s0 is not None)
    return pl.pallas_call(
        kern,
        grid=grid,
        in_specs=in_specs,
        out_specs=[pl.BlockSpec((blk_rows, w), lambda b, h, t: (row_blk(b, h, t), h)),
                   state_spec],
        out_shape=[jax.ShapeDtypeStruct((total, d_a), BF16),
                   jax.ShapeDtypeStruct((n_seq, n_heads, dk, dk), F32)],
        scratch_shapes=[pltpu.VMEM((nseq_blk * hb, dk, dk), F32), pltpu.VMEM((hb, rows, dk), F32)],
        compiler_params=_params("parallel", "parallel", "arbitrary"),
        name="hgrn_scan",
    )(*args)


HALO = 32


def _conv_kernel(*refs, tb, n_sub, conv_w, has_buf, row_sub, lane_sub):
    if has_buf:
        a_ref, b_ref, w_ref, cb_ref, lg_ref, lbias_ref, buf_ref, o_ref, bufnew_ref, up, acc = refs
    else:
        a_ref, b_ref, w_ref, cb_ref, lg_ref, lbias_ref, o_ref, bufnew_ref, up, acc = refs
    t = pl.program_id(1)
    hist = conv_w - 1
    pad = HALO - hist
    d_b = a_ref.shape[1]

    for sq in range(n_sub):
        rows = slice(sq * tb, (sq + 1) * tb)

        @pl.when(t == 0)
        def _():
            up[0:HALO, :] = jnp.zeros((HALO, d_b), F32)
            up[HALO + tb:HALO + tb + SUBLANES, :] = jnp.zeros((SUBLANES, d_b), F32)
            if has_buf:
                up[pad:HALO, :] = buf_ref[sq]

        @pl.when(t > 0)
        def _():
            up[0:HALO, :] = up[tb:tb + HALO, :]

        up[HALO:HALO + tb, :] = a_ref[rows, :] * jax.nn.sigmoid(b_ref[rows, :])

        for r0 in range(0, tb, row_sub):
            for c0 in range(0, d_b, lane_sub):
                cs = slice(c0, c0 + lane_sub)
                y = jnp.broadcast_to(cb_ref[:, cs], (row_sub, lane_sub))
                for phase in range(SUBLANES):
                    n_rows = row_sub + (SUBLANES if phase else 0)
                    part = None
                    for j in range(conv_w):
                        if (pad + j) % SUBLANES != phase:
                            continue
                        base = r0 + pad + j - phase
                        term = w_ref[j:j + 1, cs] * up[base:base + n_rows, cs]
                        part = term if part is None else part + term
                    if part is not None:
                        y = y + part[phase:phase + row_sub]
                acc[r0:r0 + row_sub, cs] = y

        c = acc[...]
        mu = jnp.mean(c, axis=-1, keepdims=True)
        xc = c - mu
        y = xc * lax.rsqrt(jnp.mean(xc * xc, axis=-1, keepdims=True) + EPS) * lg_ref[...] + lbias_ref[...]
        o_ref[rows, :] = (y * jax.nn.sigmoid(y)).astype(o_ref.dtype)

        @pl.when(t == pl.num_programs(1) - 1)
        def _():
            bufnew_ref[sq] = up[tb + pad:tb + HALO, :]


def _conv_module(z, conv_w, conv_b, ln_g, ln_b, *, n_seq, seq_len, col_blk_a, d_b, buf):
    width = conv_w.shape[0]
    hist = width - 1
    assert hist <= HALO
    tb = min(seq_len, 128)
    n_t = seq_len // tb
    assert seq_len % tb == 0
    n_sub = math.gcd(n_seq, LANES // tb) if n_t == 1 else 1
    row_sub = min(tb, 64)
    rb = lambda b, t: b * n_t + t
    vec = lambda: pl.BlockSpec((1, d_b), lambda b, t: (0, 0))
    buf_spec = pl.BlockSpec((n_sub, hist, d_b), lambda b, t: (b, 0, 0))
    in_specs = [pl.BlockSpec((n_sub * tb, d_b), lambda b, t: (rb(b, t), col_blk_a)),
                pl.BlockSpec((n_sub * tb, d_b), lambda b, t: (rb(b, t), col_blk_a + 1)),
                pl.BlockSpec((width, d_b), lambda b, t: (0, 0)), vec(), vec(), vec()]
    args = [z, z, conv_w, conv_b.reshape(1, d_b), ln_g.reshape(1, d_b), ln_b.reshape(1, d_b)]
    if buf is not None:
        in_specs.append(buf_spec)
        args.append(buf)
    kern = functools.partial(_conv_kernel, tb=tb, n_sub=n_sub, conv_w=width, has_buf=buf is not None,
                             row_sub=row_sub, lane_sub=LANES)
    return pl.pallas_call(
        kern,
        grid=(n_seq // n_sub, n_t),
        in_specs=in_specs,
        out_specs=[pl.BlockSpec((n_sub * tb, d_b), lambda b, t: (rb(b, t), 0)), buf_spec],
        out_shape=[jax.ShapeDtypeStruct((n_seq * seq_len, d_b), BF16),
                   jax.ShapeDtypeStruct((n_seq, hist, d_b), F32)],
        scratch_shapes=[pltpu.VMEM((HALO + tb + SUBLANES, d_b), F32), pltpu.VMEM((tb, d_b), F32)],
        compiler_params=_params("parallel", "arbitrary"),
        name="conv_module",
    )(*args)


def _top_desc(s, k, with_rank=False):
    vals = []
    rank = jnp.full(s.shape, float(k), F32)
    for i in range(k):
        m = jnp.max(s, axis=0, keepdims=True)
        vals.append(m)
        top = s == m
        if with_rank:
            rank = jnp.where(top, float(i), rank)
        s = jnp.where(top, -jnp.inf, s)
    vals = jnp.concatenate(vals, axis=0)
    return (vals, rank) if with_rank else vals


def _route_kernel(qh_ref, keys_ref, n1_ref, c1_ref, r2_ref, e2_ref, *, n_heads, half):
    k = PEER_TOPK
    for h in range(n_heads):
        sc = []
        for c in range(2):
            qblk = qh_ref[:, (2 * h + c) * half:(2 * h + c + 1) * half].astype(BF16)
            sc.append(lax.dot_general(keys_ref[h, c], qblk, NT_DIMS, preferred_element_type=F32))
        s1, s2 = sc
        v1 = _top_desc(s1, k)
        v2, rank2 = _top_desc(s2, k, with_rank=True)
        sums = [v1[a:a + 1] + v2 for a in range(k)]
        lim = 1
        while (lim + 1) ** 2 <= k:
            lim += 1
        rows = lax.broadcasted_iota(jnp.int32, v1.shape, 0)
        cand = sums[:lim] + [jnp.where(rows >= lim, v1 + v2[b:b + 1], -jnp.inf) for b in range(lim)]
        top = _top_desc(jnp.concatenate(cand, axis=0), k)
        tau = top[k - 1:k]
        zsum = jnp.sum(jnp.exp(top - top[0:1]), axis=0, keepdims=True)
        n1 = jnp.zeros_like(s1)
        for a in range(k):
            n_a = jnp.sum(jnp.where(sums[a] >= tau, 1.0, 0.0), axis=0, keepdims=True)
            n1 = jnp.where(s1 == v1[a:a + 1], n_a, n1)
        n1_ref[h] = n1
        c1_ref[h] = jnp.exp(s1 - v1[0:1]) / zsum
        r2_ref[h] = rank2.astype(r2_ref.dtype)
        e2_ref[h] = jnp.exp(s2 - v2[0:1]).astype(e2_ref.dtype)


def _peer_route(qh, keys, *, tt=256):
    t = qh.shape[0]
    n_heads, _, n_keys, half = keys.shape
    spec = pl.BlockSpec((n_heads, n_keys, tt), lambda i: (0, 0, i))
    f32 = jax.ShapeDtypeStruct((n_heads, n_keys, t), F32)
    b16 = jax.ShapeDtypeStruct((n_heads, n_keys, t), BF16)
    return pl.pallas_call(
        functools.partial(_route_kernel, n_heads=n_heads, half=half),
        grid=(t // tt,),
        in_specs=[pl.BlockSpec((tt, qh.shape[1]), lambda i: (i, 0)),
                  pl.BlockSpec(keys.shape, lambda i: (0, 0, 0, 0))],
        out_specs=[spec, spec, spec, spec],
        out_shape=[f32, f32, b16, b16],
        compiler_params=_params("parallel"),
        name="peer_route",
    )(qh, keys)


BF16_ROWS = 2 * SUBLANES
GATE_ROWS = 32


def _peer_act_kernel(x_ref, u_ref, n1_ref, c1_ref, r2_ref, e2_ref, w_ref, nb_scr, cb_scr, *,
                     n_heads, n_keys):
    te = u_ref.shape[0]
    tt = x_ref.shape[0]
    j = pl.program_id(1)
    hid = lax.dot_general(u_ref[...], x_ref[...], NT_DIMS, preferred_element_type=F32)
    act = hid * (0.5 + 0.5 * lax.erf(hid * (2.0 ** -0.5)))
    for ii in range(te // n_keys):
        i1 = j * (te // n_keys) + ii
        for h in range(n_heads):
            nb_scr[h] = jnp.broadcast_to(n1_ref[h, pl.ds(i1, 1), :], (BF16_ROWS, tt)).astype(BF16)
            cb_scr[h] = jnp.broadcast_to(c1_ref[h, pl.ds(i1, 1), :], (BF16_ROWS, tt)).astype(BF16)
        for r0 in range(0, n_keys, GATE_ROWS):
            rs = slice(r0, r0 + GATE_ROWS)
            gate = None
            for h in range(n_heads):
                nb = jnp.concatenate([nb_scr[h]] * (GATE_ROWS // BF16_ROWS), axis=0)
                cb = jnp.concatenate([cb_scr[h]] * (GATE_ROWS // BF16_ROWS), axis=0)
                term = jnp.where(r2_ref[h, rs, :] < nb, e2_ref[h, rs, :] * cb, jnp.zeros((), BF16))
                gate = term if gate is None else gate + term
            rows = slice(ii * n_keys + r0, ii * n_keys + r0 + GATE_ROWS)
            w_ref[rows, :] = gate * act[rows].astype(BF16)


def _peer_act(xn, u, n1, c1, r2, e2, *, tt=512, te=1024):
    t, d = xn.shape
    n_exp = u.shape[0]
    n_heads, n_keys, _ = n1.shape
    big_spec = pl.BlockSpec((n_heads, n_keys, tt), lambda i, j: (0, 0, i))
    return pl.pallas_call(
        functools.partial(_peer_act_kernel, n_heads=n_heads, n_keys=n_keys),
        grid=(t // tt, n_exp // te),
        in_specs=[pl.BlockSpec((tt, d), lambda i, j: (i, 0)),
                  pl.BlockSpec((te, d), lambda i, j: (j, 0)),
                  big_spec, big_spec, big_spec, big_spec],
        out_specs=pl.BlockSpec((te, tt), lambda i, j: (j, i)),
        out_shape=jax.ShapeDtypeStruct((n_exp, t), BF16),
        scratch_shapes=[pltpu.VMEM((n_heads, BF16_ROWS, tt), BF16),
                        pltpu.VMEM((n_heads, BF16_ROWS, tt), BF16)],
        compiler_params=_params("parallel", "arbitrary"),
        name="peer_act",
    )(xn, u, n1, c1, r2, e2)


def _peer_out_kernel(w_ref, v_ref, r_ref, g_ref, o_ref, n_ref):
    k = pl.program_id(1)

    @pl.when(k == 0)
    def _():
        o_ref[...] = r_ref[...]

    o_ref[...] += lax.dot_general(w_ref[...], v_ref[...], TN_DIMS, preferred_element_type=F32)

    @pl.when(k == pl.num_programs(1) - 1)
    def _():
        x = o_ref[...]
        inv = lax.rsqrt(jnp.mean(x * x, axis=-1, keepdims=True) + EPS)
        n_ref[...] = (x * inv * g_ref[...]).astype(n_ref.dtype)


def _peer_out(wt, v, res, norm_g, *, tt=512, tk=1024):
    n_exp, t = wt.shape
    d = v.shape[1]
    row_spec = pl.BlockSpec((tt, d), lambda i, k: (i, 0))
    return pl.pallas_call(
        _peer_out_kernel,
        grid=(t // tt, n_exp // tk),
        in_specs=[pl.BlockSpec((tk, tt), lambda i, k: (k, i)),
                  pl.BlockSpec((tk, d), lambda i, k: (k, 0)),
                  pl.BlockSpec((tt, d), lambda i, k: (i, 0), pipeline_mode=pl.Buffered(1)),
                  pl.BlockSpec((1, d), lambda i, k: (0, 0))],
        out_specs=[row_spec, row_spec],
        out_shape=[jax.ShapeDtypeStruct((t, d), F32), jax.ShapeDtypeStruct((t, d), BF16)],
        compiler_params=_params("parallel", "arbitrary"),
        name="peer_out",
    )(wt, v, res, norm_g.reshape(1, d))


def _layer(x, p, n_seq, seq_len, s0, buf, lb, norm1_g, w_in, hgrn_norm_g, conv_w, conv_b, conv_ln_g,
           conv_ln_b, w_out_a, w_out_b, norm2_g, peer_wq, peer_keys, peer_u, peer_v, ple_norm_g, ple_wg,
           ple_wp):
    d_a = lb.shape[0]
    d_b = conv_w.shape[1]
    hn = _rmsnorm(x, norm1_g, BF16)
    z = _matmul([(hn, w_in)])
    o_a, s_new = _hgrn_scan(z, lb, hgrn_norm_g, n_seq=n_seq, seq_len=seq_len, d_a=d_a, s0=s0)
    o_b, buf_new = _conv_module(z, conv_w, conv_b, conv_ln_g, conv_ln_b, n_seq=n_seq, seq_len=seq_len,
                                col_blk_a=4 * d_a // d_b, d_b=d_b, buf=buf)
    x = _matmul([(o_a, w_out_a), (o_b, w_out_b)], res=x)
    hn2 = _rmsnorm(x, norm2_g, BF16)
    qh = _matmul([(hn2, peer_wq)])
    n1, c1, r2, e2 = _peer_route(qh, peer_keys)
    wt = _peer_act(hn2, peer_u, n1, c1, r2, e2)
    x, hn3 = _peer_out(wt, peer_v, x, ple_norm_g)
    x = _matmul([(hn3, ple_wg)], res=x, ple=(p, ple_wp))
    return x, s_new, buf_new


def kernel(x_prompt, x_sample, state_hgrn, state_conv, p_prompt, p_sample, lb_logits, norm1_g, w_in,
           hgrn_norm_g, conv_w, conv_b, conv_ln_g, conv_ln_b, w_out, norm2_g, peer_wq, peer_keys,
           peer_u, peer_v, ple_norm_g, ple_wg, ple_wp, final_g):
    n_p, l_p, d = x_prompt.shape
    n_s, l_s, _ = x_sample.shape
    depth = w_in.shape[0]
    d_a = lb_logits.shape[1]
    lb_all = jnp.cumsum(jax.nn.softmax(lb_logits.astype(F32), axis=0), axis=0)
    hp = x_prompt.reshape(n_p * l_p, d)
    hs = x_sample.reshape(n_s * l_s, d)
    sp_l, cp_l, ss_l, cs_l = [], [], [], []
    for i in range(depth):
        w = (lb_all[i], norm1_g[i], w_in[i].astype(BF16), hgrn_norm_g[i], conv_w[i], conv_b[i],
             conv_ln_g[i], conv_ln_b[i], w_out[i, :d_a].astype(BF16), w_out[i, d_a:].astype(BF16),
             norm2_g[i], peer_wq[i].astype(BF16), peer_keys[i].astype(BF16), peer_u[i].astype(BF16),
             peer_v[i].astype(BF16), ple_norm_g[i], ple_wg[i].astype(BF16), ple_wp[i].astype(BF16))
        hp, sp, cp = _layer(hp, p_prompt[i].reshape(n_p * l_p, -1).astype(BF16), n_p, l_p, None, None, *w)
        hs, ss, cs = _layer(hs, p_sample[i].reshape(n_s * l_s, -1).astype(BF16), n_s, l_s,
                            state_hgrn[i], state_conv[i], *w)
        sp_l.append(sp.astype(state_hgrn.dtype))
        cp_l.append(cp.astype(state_conv.dtype))
        ss_l.append(ss.astype(state_hgrn.dtype))
        cs_l.append(cs.astype(state_conv.dtype))
    y_prompt = _rmsnorm(hp, final_g, x_prompt.dtype).reshape(n_p, l_p, d)
    y_sample = _rmsnorm(hs, final_g, x_sample.dtype).reshape(n_s, l_s, d)
    return (y_prompt, y_sample, jnp.stack(sp_l), jnp.stack(cp_l), jnp.stack(ss_l), jnp.stack(cs_l))
```

```python
import functools
import math

import numpy as np
import jax
import jax.numpy as jnp
from jax import lax
from jax.experimental import pallas as pl
from jax.experimental.pallas import tpu as pltpu

EPS = 1e-6
HGRN_CHUNK = 128
PEER_TOPK = 16
LANES = 128
SUBLANES = 8
VMEM_LIMIT = 60 * 1024 * 1024
F32 = jnp.float32
BF16 = jnp.bfloat16
NT_DIMS = (((1,), (1,)), ((), ()))
TN_DIMS = (((0,), (0,)), ((), ()))


def _params(*sem):
    return pltpu.CompilerParams(dimension_semantics=sem, vmem_limit_bytes=VMEM_LIMIT)


def _rmsnorm_kernel(x_ref, g_ref, o_ref):
    x = x_ref[...]
    inv = lax.rsqrt(jnp.mean(x * x, axis=-1, keepdims=True) + EPS)
    o_ref[...] = (x * inv * g_ref[...]).astype(o_ref.dtype)


def _rmsnorm(x, g, out_dtype, *, tm=512):
    t, d = x.shape
    assert t % tm == 0
    return pl.pallas_call(
        _rmsnorm_kernel,
        grid=(t // tm,),
        in_specs=[pl.BlockSpec((tm, d), lambda i: (i, 0)),
                  pl.BlockSpec((1, d), lambda i: (0, 0))],
        out_specs=pl.BlockSpec((tm, d), lambda i: (i, 0)),
        out_shape=jax.ShapeDtypeStruct((t, d), out_dtype),
        compiler_params=_params("parallel"),
        name="rmsnorm",
    )(x, g.reshape(1, d))


def _mm_kernel(*refs, n_lhs, has_ple, has_res):
    o_ref = refs[-1]
    acc = None
    for k in range(n_lhs):
        part = jnp.dot(refs[2 * k][...], refs[2 * k + 1][...], preferred_element_type=F32)
        acc = part if acc is None else acc + part
    nxt = 2 * n_lhs
    if has_ple:
        proj = jnp.dot(refs[nxt][...], refs[nxt + 1][...], preferred_element_type=F32)
        acc = jax.nn.sigmoid(acc) * proj
        nxt += 2
    if has_res:
        acc = refs[nxt][...] + acc
    o_ref[...] = acc


def _matmul(pairs, *, res=None, ple=None, tm=1024, tn=1024):
    m = pairs[0][0].shape[0]
    n = pairs[0][1].shape[1]
    assert m % tm == 0 and n % tn == 0
    in_specs, args = [], []
    for a, w in pairs + ([ple] if ple is not None else []):
        in_specs += [pl.BlockSpec((tm, a.shape[1]), lambda i, j: (i, 0)),
                     pl.BlockSpec((w.shape[0], tn), lambda i, j: (0, j))]
        args += [a, w]
    if res is not None:
        in_specs.append(pl.BlockSpec((tm, tn), lambda i, j: (i, j)))
        args.append(res)
    return pl.pallas_call(
        functools.partial(_mm_kernel, n_lhs=len(pairs), has_ple=ple is not None, has_res=res is not None),
        grid=(m // tm, n // tn),
        in_specs=in_specs,
        out_specs=pl.BlockSpec((tm, tn), lambda i, j: (i, j)),
        out_shape=jax.ShapeDtypeStruct((m, n), F32),
        compiler_params=_params("parallel", "parallel"),
        name="matmul",
    )(*args)


def _scan_consts(rows, seq_len):
    r = np.arange(rows)
    seq = r // seq_len
    cum = (r[None, :] <= r[:, None]) & (seq[:, None] == seq[None, :])
    masks = []
    m = seq_len
    while m >= 2:
        upper = r % m >= m // 2
        same_blk = (r // m)[:, None] == (r // m)[None, :]
        masks.append(same_blk & upper[:, None] & (~upper)[None, :])
        m //= 2
    cmat = np.tile(cum, (1, 3)).astype(np.float32)
    masks = np.stack(masks).astype(np.float32)
    eye = np.eye(rows, dtype=np.float32)
    sel = np.tile(np.repeat(seq[:, None] == np.arange(rows // seq_len)[None, :], LANES, axis=1), (3, 1))
    return cmat, masks, eye, sel.astype(np.float32)


def _scan_kernel(*refs, rows, nseq, n_chunks, hb, n_levels, has_s0):
    if has_s0:
        (q_ref, f_ref, i_ref, g_ref, lb_ref, ng_ref, cmat_ref, mask_ref, eye_ref, sel_ref, s0_ref,
         o_ref, sfin_ref, s_scr, g_scr) = refs
    else:
        (q_ref, f_ref, i_ref, g_ref, lb_ref, ng_ref, cmat_ref, mask_ref, eye_ref, sel_ref,
         o_ref, sfin_ref, s_scr, g_scr) = refs
    t = pl.program_id(2)
    dk = LANES
    seq_len = rows // nseq
    sub_row = lax.broadcasted_iota(jnp.int32, (SUBLANES, dk), 0)

    @pl.when(t == 0)
    def _():
        if has_s0:
            for j in range(nseq):
                for hh in range(hb):
                    s_scr[j * hb + hh] = s0_ref[j, hh]
        else:
            s_scr[...] = jnp.zeros_like(s_scr)

    def chunk(ci, carry):
        r0 = pl.multiple_of(ci * rows, rows)
        for hh in range(hb):
            cols = slice(hh * dk, (hh + 1) * dk)
            qr = q_ref[pl.ds(r0, rows), cols]
            fr = f_ref[pl.ds(r0, rows), cols]
            v = i_ref[pl.ds(r0, rows), cols]
            gr = g_ref[pl.ds(r0, rows), cols]
            lb = lb_ref[:, cols]
            f = lb + (1.0 - lb) * jax.nn.sigmoid(fr)
            logf = jnp.log(f)
            kk = 1.0 - f
            q = qr * jax.nn.sigmoid(qr)
            hi = logf.astype(BF16)
            r1 = logf - hi.astype(F32)
            mid = r1.astype(BF16)
            lo = (r1 - mid.astype(F32)).astype(BF16)
            pieces = jnp.concatenate([hi, mid, lo], axis=0)
            g = jnp.dot(cmat_ref[...], pieces, preferred_element_type=F32)
            g_last = lax.dot_general(pieces, sel_ref[...], TN_DIMS,
                                     preferred_element_type=F32)
            g_scr[hh] = g
            g_row = lambda i: g_scr[hh, i:i + 1, :]
            tiles = [g[r8:r8 + SUBLANES] for r8 in range(0, rows, SUBLANES)]
            to_end = [g_row((r8 // seq_len + 1) * seq_len - 1) - tiles[r8 // SUBLANES]
                      for r8 in range(0, rows, SUBLANES)]
            kdec = kk * jnp.exp(jnp.concatenate(to_end, axis=0))
            qdec = q * jnp.exp(g)
            vb = v.astype(BF16)
            a = eye_ref[...] * jnp.sum(q * kk, axis=1, keepdims=True)
            q_tiles = [q[r8:r8 + SUBLANES] for r8 in range(0, rows, SUBLANES)]
            k_tiles = [kk[r8:r8 + SUBLANES] for r8 in range(0, rows, SUBLANES)]
            for l in range(n_levels):
                m = seq_len >> l
                half = m // 2
                parts = []
                for r8 in range(0, rows, SUBLANES):
                    refs = sorted({r - r % m + half - 1 for r in range(r8, r8 + SUBLANES)})
                    g_ref_rows = g_row(refs[-1])
                    for ref in reversed(refs[:-1]):
                        g_ref_rows = jnp.where(sub_row <= ref + half - r8, g_row(ref), g_ref_rows)
                    ti = r8 // SUBLANES
                    if half >= SUBLANES:
                        upper = r8 % m >= half
                        d = tiles[ti] - g_ref_rows if upper else g_ref_rows - tiles[ti]
                        x = q_tiles[ti] if upper else k_tiles[ti]
                    else:
                        d = -jnp.abs(tiles[ti] - g_ref_rows)
                        x = jnp.where(sub_row % m >= half, q_tiles[ti], k_tiles[ti])
                    parts.append(x * jnp.exp(d))
                xe = jnp.concatenate(parts, axis=0).astype(BF16)
                al = lax.dot_general(xe, xe, NT_DIMS, preferred_element_type=F32)
                a = jnp.where(mask_ref[l] != 0.0, al, a)
            o = jnp.dot(a.astype(BF16), vb, preferred_element_type=F32)
            o_inter = []
            for j in range(nseq):
                rs = slice(j * seq_len, (j + 1) * seq_len)
                s = s_scr[j * hb + hh]
                o_inter.append(jnp.dot(qdec[rs].astype(BF16), s.astype(BF16),
                                       preferred_element_type=F32))
                upd = lax.dot_general(kdec[rs].astype(BF16), v[rs].astype(BF16), TN_DIMS,
                                      preferred_element_type=F32)
                s_scr[j * hb + hh] = jnp.exp(g_last[:, j * dk:(j + 1) * dk]) * s + upd
            o = o + (o_inter[0] if nseq == 1 else jnp.concatenate(o_inter, axis=0))
            on = o * lax.rsqrt(jnp.mean(o * o, axis=-1, keepdims=True) + EPS) * ng_ref[:, cols]
            o_ref[pl.ds(r0, rows), cols] = (on * (gr * jax.nn.sigmoid(gr))).astype(o_ref.dtype)
        return carry

    if n_chunks == 1:
        chunk(0, 0)
    else:
        lax.fori_loop(0, n_chunks, chunk, 0)

    @pl.when(t == pl.num_programs(2) - 1)
    def _():
        for j in range(nseq):
            for hh in range(hb):
                sfin_ref[j, hh] = s_scr[j * hb + hh]


def _hgrn_scan(z, lb, ng, *, n_seq, seq_len, d_a, s0):
    dk = LANES
    n_heads = d_a // dk
    sec = d_a // dk
    if s0 is None:
        rows = math.gcd(seq_len, HGRN_CHUNK)
        nseq_blk = 1
        hb = min(8, n_heads)
        tbk = min(seq_len, 512)
        n_t = seq_len // tbk
        grid = (n_seq, n_heads // hb, n_t)
        n_chunks = tbk // rows
        blk_rows = tbk
        row_blk = lambda b, h, t: b * n_t + t
        consts = _scan_consts(rows, rows)
    else:
        rows = LANES
        nseq_blk = rows // seq_len
        hb = min(4, n_heads)
        grid = (n_seq // nseq_blk, n_heads // hb, 1)
        n_chunks = 1
        blk_rows = rows
        row_blk = lambda b, h, t: b
        consts = _scan_consts(rows, seq_len)
    cmat, masks, eye, sel = (jnp.asarray(c) for c in consts)
    cmat, sel = cmat.astype(BF16), sel.astype(BF16)
    n_levels = masks.shape[0]
    w = hb * dk

    def zspec(section):
        return pl.BlockSpec((blk_rows, w), lambda b, h, t: (row_blk(b, h, t), section * (sec // hb) + h))

    vec_spec = pl.BlockSpec((1, w), lambda b, h, t: (0, h))
    full = lambda arr: pl.BlockSpec(arr.shape, lambda b, h, t: (0,) * arr.ndim)
    state_spec = pl.BlockSpec((nseq_blk, hb, dk, dk), lambda b, h, t: (b, h, 0, 0))
    in_specs = [zspec(0), zspec(1), zspec(2), zspec(3), vec_spec, vec_spec,
                full(cmat), full(masks), full(eye), full(sel)]
    args = [z, z, z, z, lb.reshape(1, d_a), ng.reshape(1, d_a), cmat, masks, eye, sel]
    if s0 is not None:
        in_specs.append(state_spec)
        args.append(s0)
    total = n_seq * seq_len
    kern = functools.partial(_scan_kernel, rows=rows, nseq=nseq_blk, n_chunks=n_chunks, hb=hb,
                             n_levels=n_levels, has_s0=s0 is not None)
    return pl.pallas_call(
        kern,
        grid=grid,
        in_specs=in_specs,
        out_specs=[pl.BlockSpec((blk_rows, w), lambda b, h, t: (row_blk(b, h, t), h)),
                   state_spec],
        out_shape=[jax.ShapeDtypeStruct((total, d_a), BF16),
                   jax.ShapeDtypeStruct((n_seq, n_heads, dk, dk), F32)],
        scratch_shapes=[pltpu.VMEM((nseq_blk * hb, dk, dk), F32), pltpu.VMEM((hb, rows, dk), F32)],
        compiler_params=_params("parallel", "parallel", "arbitrary"),
        name="hgrn_scan",
    )(*args)


HALO = 32


def _conv_kernel(*refs, tb, n_sub, conv_w, has_buf, row_sub, lane_sub):
    if has_buf:
        a_ref, b_ref, w_ref, cb_ref, lg_ref, lbias_ref, buf_ref, o_ref, bufnew_ref, up, acc = refs
    else:
        a_ref, b_ref, w_ref, cb_ref, lg_ref, lbias_ref, o_ref, bufnew_ref, up, acc = refs
    t = pl.program_id(1)
    hist = conv_w - 1
    pad = HALO - hist
    d_b = a_ref.shape[1]

    for sq in range(n_sub):
        rows = slice(sq * tb, (sq + 1) * tb)

        @pl.when(t == 0)
        def _():
            up[0:HALO, :] = jnp.zeros((HALO, d_b), F32)
            up[HALO + tb:HALO + tb + SUBLANES, :] = jnp.zeros((SUBLANES, d_b), F32)
            if has_buf:
                up[pad:HALO, :] = buf_ref[sq]

        @pl.when(t > 0)
        def _():
            up[0:HALO, :] = up[tb:tb + HALO, :]

        up[HALO:HALO + tb, :] = a_ref[rows, :] * jax.nn.sigmoid(b_ref[rows, :])

        for r0 in range(0, tb, row_sub):
            for c0 in range(0, d_b, lane_sub):
                cs = slice(c0, c0 + lane_sub)
                y = jnp.broadcast_to(cb_ref[:, cs], (row_sub, lane_sub))
                for phase in range(SUBLANES):
                    n_rows = row_sub + (SUBLANES if phase else 0)
                    part = None
                    for j in range(conv_w):
                        if (pad + j) % SUBLANES != phase:
                            continue
                        base = r0 + pad + j - phase
                        term = w_ref[j:j + 1, cs] * up[base:base + n_rows, cs]
                        part = term if part is None else part + term
                    if part is not None:
                        y = y + part[phase:phase + row_sub]
                acc[r0:r0 + row_sub, cs] = y

        c = acc[...]
        mu = jnp.mean(c, axis=-1, keepdims=True)
        xc = c - mu
        y = xc * lax.rsqrt(jnp.mean(xc * xc, axis=-1, keepdims=True) + EPS) * lg_ref[...] + lbias_ref[...]
        o_ref[rows, :] = (y * jax.nn.sigmoid(y)).astype(o_ref.dtype)

        @pl.when(t == pl.num_programs(1) - 1)
        def _():
            bufnew_ref[sq] = up[tb + pad:tb + HALO, :]


def _conv_module(z, conv_w, conv_b, ln_g, ln_b, *, n_seq, seq_len, col_blk_a, d_b, buf):
    width = conv_w.shape[0]
    hist = width - 1
    assert hist <= HALO
    tb = min(seq_len, 128)
    n_t = seq_len // tb
    assert seq_len % tb == 0
    n_sub = math.gcd(n_seq, LANES // tb) if n_t == 1 else 1
    row_sub = min(tb, 64)
    rb = lambda b, t: b * n_t + t
    vec = lambda: pl.BlockSpec((1, d_b), lambda b, t: (0, 0))
    buf_spec = pl.BlockSpec((n_sub, hist, d_b), lambda b, t: (b, 0, 0))
    in_specs = [pl.BlockSpec((n_sub * tb, d_b), lambda b, t: (rb(b, t), col_blk_a)),
                pl.BlockSpec((n_sub * tb, d_b), lambda b, t: (rb(b, t), col_blk_a + 1)),
                pl.BlockSpec((width, d_b), lambda b, t: (0, 0)), vec(), vec(), vec()]
    args = [z, z, conv_w, conv_b.reshape(1, d_b), ln_g.reshape(1, d_b), ln_b.reshape(1, d_b)]
    if buf is not None:
        in_specs.append(buf_spec)
        args.append(buf)
    kern = functools.partial(_conv_kernel, tb=tb, n_sub=n_sub, conv_w=width, has_buf=buf is not None,
                             row_sub=row_sub, lane_sub=LANES)
    return pl.pallas_call(
        kern,
        grid=(n_seq // n_sub, n_t),
        in_specs=in_specs,
        out_specs=[pl.BlockSpec((n_sub * tb, d_b), lambda b, t: (rb(b, t), 0)), buf_spec],
        out_shape=[jax.ShapeDtypeStruct((n_seq * seq_len, d_b), BF16),
                   jax.ShapeDtypeStruct((n_seq, hist, d_b), F32)],
        scratch_shapes=[pltpu.VMEM((HALO + tb + SUBLANES, d_b), F32), pltpu.VMEM((tb, d_b), F32)],
        compiler_params=_params("parallel", "arbitrary"),
        name="conv_module",
    )(*args)


def _top_desc(s, k, with_rank=False):
    vals = []
    rank = jnp.full(s.shape, float(k), F32)
    for i in range(k):
        m = jnp.max(s, axis=0, keepdims=True)
        vals.append(m)
        top = s == m
        if with_rank:
            rank = jnp.where(top, float(i), rank)
        s = jnp.where(top, -jnp.inf, s)
    vals = jnp.concatenate(vals, axis=0)
    return (vals, rank) if with_rank else vals


def _route_kernel(qh_ref, keys_ref, n1_ref, c1_ref, r2_ref, e2_ref, *, n_heads, half):
    k = PEER_TOPK
    for h in range(n_heads):
        sc = []
        for c in range(2):
            qblk = qh_ref[:, (2 * h + c) * half:(2 * h + c + 1) * half].astype(BF16)
            sc.append(lax.dot_general(keys_ref[h, c], qblk, NT_DIMS, preferred_element_type=F32))
        s1, s2 = sc
        v1 = _top_desc(s1, k)
        v2, rank2 = _top_desc(s2, k, with_rank=True)
        sums = [v1[a:a + 1] + v2 for a in range(k)]
        lim = 1
        while (lim + 1) ** 2 <= k:
            lim += 1
        tile_up = lambda n: min(k, -(-n // SUBLANES) * SUBLANES)
        low = lax.broadcasted_iota(jnp.int32, (SUBLANES, v1.shape[1]), 0) < lim % SUBLANES
        cand = [sums[a][:tile_up(k // (a + 1))] for a in range(lim)]
        for b in range(lim):
            col = v1 + v2[b:b + 1]
            n_a = tile_up(k // (b + 1))
            first = lim // SUBLANES * SUBLANES
            if n_a > first:
                head = col[first:first + SUBLANES]
                cand.append(jnp.where(low, -jnp.inf, head) if lim % SUBLANES else head)
                if n_a > first + SUBLANES:
                    cand.append(col[first + SUBLANES:n_a])
        top = _top_desc(jnp.concatenate(cand, axis=0), k)
        tau = top[k - 1:k]
        zsum = jnp.sum(jnp.exp(top - top[0:1]), axis=0, keepdims=True)
        n1 = jnp.zeros_like(s1)
        for a in range(k):
            n_a = jnp.sum(jnp.where(sums[a] >= tau, 1.0, 0.0), axis=0, keepdims=True)
            n1 = jnp.where(s1 == v1[a:a + 1], n_a, n1)
        n1_ref[h] = n1
        c1_ref[h] = jnp.exp(s1 - v1[0:1]) / zsum
        r2_ref[h] = rank2.astype(r2_ref.dtype)
        e2_ref[h] = jnp.exp(s2 - v2[0:1]).astype(e2_ref.dtype)


def _peer_route(qh, keys, *, tt=256):
    t = qh.shape[0]
    n_heads, _, n_keys, half = keys.shape
    spec = pl.BlockSpec((n_heads, n_keys, tt), lambda i: (0, 0, i))
    f32 = jax.ShapeDtypeStruct((n_heads, n_keys, t), F32)
    b16 = jax.ShapeDtypeStruct((n_heads, n_keys, t), BF16)
    return pl.pallas_call(
        functools.partial(_route_kernel, n_heads=n_heads, half=half),
        grid=(t // tt,),
        in_specs=[pl.BlockSpec((tt, qh.shape[1]), lambda i: (i, 0)),
                  pl.BlockSpec(keys.shape, lambda i: (0, 0, 0, 0))],
        out_specs=[spec, spec, spec, spec],
        out_shape=[f32, f32, b16, b16],
        compiler_params=_params("parallel"),
        name="peer_route",
    )(qh, keys)


BF16_ROWS = 2 * SUBLANES
GATE_ROWS = 32


def _peer_act_kernel(x_ref, u_ref, n1_ref, c1_ref, r2_ref, e2_ref, w_ref, nb_scr, cb_scr, *,
                     n_heads, n_keys):
    te = u_ref.shape[0]
    tt = x_ref.shape[0]
    j = pl.program_id(1)
    hid = lax.dot_general(u_ref[...], x_ref[...], NT_DIMS, preferred_element_type=F32)
    act = hid * (0.5 + 0.5 * lax.erf(hid * (2.0 ** -0.5)))
    for ii in range(te // n_keys):
        i1 = j * (te // n_keys) + ii
        for h in range(n_heads):
            nb_scr[h] = jnp.broadcast_to(n1_ref[h, pl.ds(i1, 1), :], (BF16_ROWS, tt)).astype(BF16)
            cb_scr[h] = jnp.broadcast_to(c1_ref[h, pl.ds(i1, 1), :], (BF16_ROWS, tt)).astype(BF16)
        for r0 in range(0, n_keys, GATE_ROWS):
            rs = slice(r0, r0 + GATE_ROWS)
            gate = None
            for h in range(n_heads):
                nb = jnp.concatenate([nb_scr[h]] * (GATE_ROWS // BF16_ROWS), axis=0)
                cb = jnp.concatenate([cb_scr[h]] * (GATE_ROWS // BF16_ROWS), axis=0)
                term = jnp.where(r2_ref[h, rs, :] < nb, e2_ref[h, rs, :] * cb, jnp.zeros((), BF16))
                gate = term if gate is None else gate + term
            rows = slice(ii * n_keys + r0, ii * n_keys + r0 + GATE_ROWS)
            w_ref[rows, :] = gate * act[rows].astype(BF16)


def _peer_act(xn, u, n1, c1, r2, e2, *, tt=512, te=1024):
    t, d = xn.shape
    n_exp = u.shape[0]
    n_heads, n_keys, _ = n1.shape
    big_spec = pl.BlockSpec((n_heads, n_keys, tt), lambda i, j: (0, 0, i))
    return pl.pallas_call(
        functools.partial(_peer_act_kernel, n_heads=n_heads, n_keys=n_keys),
        grid=(t // tt, n_exp // te),
        in_specs=[pl.BlockSpec((tt, d), lambda i, j: (i, 0)),
                  pl.BlockSpec((te, d), lambda i, j: (j, 0)),
                  big_spec, big_spec, big_spec, big_spec],
        out_specs=pl.BlockSpec((te, tt), lambda i, j: (j, i)),
        out_shape=jax.ShapeDtypeStruct((n_exp, t), BF16),
        scratch_shapes=[pltpu.VMEM((n_heads, BF16_ROWS, tt), BF16),
                        pltpu.VMEM((n_heads, BF16_ROWS, tt), BF16)],
        compiler_params=_params("parallel", "arbitrary"),
        name="peer_act",
    )(xn, u, n1, c1, r2, e2)


def _peer_out_kernel(w_ref, v_ref, r_ref, g_ref, o_ref, n_ref):
    k = pl.program_id(1)

    @pl.when(k == 0)
    def _():
        o_ref[...] = r_ref[...]

    o_ref[...] += lax.dot_general(w_ref[...], v_ref[...], TN_DIMS, preferred_element_type=F32)

    @pl.when(k == pl.num_programs(1) - 1)
    def _():
        x = o_ref[...]
        inv = lax.rsqrt(jnp.mean(x * x, axis=-1, keepdims=True) + EPS)
        n_ref[...] = (x * inv * g_ref[...]).astype(n_ref.dtype)


def _peer_out(wt, v, res, norm_g, *, tt=512, tk=1024):
    n_exp, t = wt.shape
    d = v.shape[1]
    row_spec = pl.BlockSpec((tt, d), lambda i, k: (i, 0))
    return pl.pallas_call(
        _peer_out_kernel,
        grid=(t // tt, n_exp // tk),
        in_specs=[pl.BlockSpec((tk, tt), lambda i, k: (k, i)),
                  pl.BlockSpec((tk, d), lambda i, k: (k, 0)),
                  pl.BlockSpec((tt, d), lambda i, k: (i, 0), pipeline_mode=pl.Buffered(1)),
                  pl.BlockSpec((1, d), lambda i, k: (0, 0))],
        out_specs=[row_spec, row_spec],
        out_shape=[jax.ShapeDtypeStruct((t, d), F32), jax.ShapeDtypeStruct((t, d), BF16)],
        compiler_params=_params("parallel", "arbitrary"),
        name="peer_out",
    )(wt, v, res, norm_g.reshape(1, d))


def _layer(x, p, n_seq, seq_len, s0, buf, lb, norm1_g, w_in, hgrn_norm_g, conv_w, conv_b, conv_ln_g,
           conv_ln_b, w_out_a, w_out_b, norm2_g, peer_wq, peer_keys, peer_u, peer_v, ple_norm_g, ple_wg,
           ple_wp):
    d_a = lb.shape[0]
    d_b = conv_w.shape[1]
    hn = _rmsnorm(x, norm1_g, BF16)
    z = _matmul([(hn, w_in)])
    o_a, s_new = _hgrn_scan(z, lb, hgrn_norm_g, n_seq=n_seq, seq_len=seq_len, d_a=d_a, s0=s0)
    o_b, buf_new = _conv_module(z, conv_w, conv_b, conv_ln_g, conv_ln_b, n_seq=n_seq, seq_len=seq_len,
                                col_blk_a=4 * d_a // d_b, d_b=d_b, buf=buf)
    x = _matmul([(o_a, w_out_a), (o_b, w_out_b)], res=x)
    hn2 = _rmsnorm(x, norm2_g, BF16)
    qh = _matmul([(hn2, peer_wq)])
    n1, c1, r2, e2 = _peer_route(qh, peer_keys)
    wt = _peer_act(hn2, peer_u, n1, c1, r2, e2)
    x, hn3 = _peer_out(wt, peer_v, x, ple_norm_g)
    x = _matmul([(hn3, ple_wg)], res=x, ple=(p, ple_wp))
    return x, s_new, buf_new


def kernel(x_prompt, x_sample, state_hgrn, state_conv, p_prompt, p_sample, lb_logits, norm1_g, w_in,
           hgrn_norm_g, conv_w, conv_b, conv_ln_g, conv_ln_b, w_out, norm2_g, peer_wq, peer_keys,
           peer_u, peer_v, ple_norm_g, ple_wg, ple_wp, final_g):
    n_p, l_p, d = x_prompt.shape
    n_s, l_s, _ = x_sample.shape
    depth = w_in.shape[0]
    d_a = lb_logits.shape[1]
    lb_all = jnp.cumsum(jax.nn.softmax(lb_logits.astype(F32), axis=0), axis=0)
    hp = x_prompt.reshape(n_p * l_p, d)
    hs = x_sample.reshape(n_s * l_s, d)
    sp_l, cp_l, ss_l, cs_l = [], [], [], []
    for i in range(depth):
        w = (lb_all[i], norm1_g[i], w_in[i].astype(BF16), hgrn_norm_g[i], conv_w[i], conv_b[i],
             conv_ln_g[i], conv_ln_b[i], w_out[i, :d_a].astype(BF16), w_out[i, d_a:].astype(BF16),
             norm2_g[i], peer_wq[i].astype(BF16), peer_keys[i].astype(BF16), peer_u[i].astype(BF16),
             peer_v[i].astype(BF16), ple_norm_g[i], ple_wg[i].astype(BF16), ple_wp[i].astype(BF16))
        hp, sp, cp = _layer(hp, p_prompt[i].reshape(n_p * l_p, -1).astype(BF16), n_p, l_p, None, None, *w)
        hs, ss, cs = _layer(hs, p_sample[i].reshape(n_s * l_s, -1).astype(BF16), n_s, l_s,
                            state_hgrn[i], state_conv[i], *w)
        sp_l.append(sp.astype(state_hgrn.dtype))
        cp_l.append(cp.astype(state_conv.dtype))
        ss_l.append(ss.astype(state_hgrn.dtype))
        cs_l.append(cs.astype(state_conv.dtype))
    y_prompt = _rmsnorm(hp, final_g, x_prompt.dtype).reshape(n_p, l_p, d)
    y_sample = _rmsnorm(hs, final_g, x_sample.dtype).reshape(n_s, l_s, d)
    return (y_prompt, y_sample, jnp.stack(sp_l), jnp.stack(cp_l), jnp.stack(ss_l), jnp.stack(cs_l))
```
